```python
import math
import jax, jax.numpy as jnp
from jax import lax
import numpy as np

D_MODEL = 1024
BATCH = 4
SEQ = 4096
DEPTH = 4
DEC_BATCH = 128
DEC_SEQ = 8
PAST_LEN = 2048
PAGE_SIZE = 128

FOX_HEADS = 8
FOX_HEAD_DIM = 64
FOX_WIDTH = FOX_HEADS * FOX_HEAD_DIM
FOX_GATE_BIAS_MEAN = 3.0
Q_BLOCK = 128
MASK_VALUE = -1e30
HG_HEADS = 4
HG_KEY_DIM = 128
HG_VAL_DIM = 128
HG_WIDTH = HG_HEADS * HG_KEY_DIM
HG_VWIDTH = HG_HEADS * HG_VAL_DIM
HG_CHUNK = 64
D_FF = 2816
CONV_WIDTH = 3
IN_WIDTH = 3 * FOX_WIDTH + FOX_HEADS + 2 * HG_WIDTH + 2 * HG_VWIDTH + 2 * D_MODEL
ALPHA = (2 * DEPTH) ** 0.25
BETA = (8 * DEPTH) ** -0.25
LN_EPS = 1e-5
RMS_EPS = 1e-6
POOL_NUM = 5
POOL_DEN = 4

kernel_name = "fox_hgrn2_gated_hybrid_step"


def _split_points():
    sizes = (FOX_WIDTH, FOX_WIDTH, FOX_WIDTH, FOX_HEADS, HG_WIDTH, HG_WIDTH,
             HG_VWIDTH, HG_VWIDTH, D_MODEL, D_MODEL)
    return [int(v) for v in np.cumsum(sizes)[:-1]]


def _layer_norm(x, g, b):
    xf = x.astype(jnp.float32)
    mu = jnp.mean(xf, axis=-1, keepdims=True)
    var = jnp.mean(jnp.square(xf - mu), axis=-1, keepdims=True)
    return ((xf - mu) * lax.rsqrt(var + LN_EPS) * g + b).astype(x.dtype)


def _fox_attention(q, k, v, d_q, d_k, q_offset):
    B, T, H, Dh = q.shape
    L = k.shape[1]
    qb = math.gcd(T, Q_BLOCK)
    dq_t = jnp.transpose(d_q, (0, 2, 1))
    dk_t = jnp.transpose(d_k, (0, 2, 1))
    k_pos = jnp.arange(L)

    def block(i):
        start = i * qb
        q_blk = lax.dynamic_slice_in_dim(q, start, qb, axis=1)
        dq_blk = lax.dynamic_slice_in_dim(dq_t, start, qb, axis=2)
        logits = jnp.einsum("bthd,bshd->bhts", q_blk, k).astype(jnp.float32) * (Dh ** -0.5)
        logits = logits + dq_blk[..., None] - dk_t[:, :, None, :]
        q_pos = q_offset + start + jnp.arange(qb)
        mask = k_pos[None, :] <= q_pos[:, None]
        p = jax.nn.softmax(jnp.where(mask, logits, MASK_VALUE), axis=-1)
        return jnp.einsum("bhts,bshd->bthd", p.astype(v.dtype), v)

    out = lax.map(block, jnp.arange(T // qb))
    return jnp.transpose(out, (1, 0, 2, 3, 4)).reshape(B, T, H, Dh)


def _hgrn2_chunked(q, k, v, log_f, s0):
    B, T, H, K = q.shape
    V = v.shape[-1]
    c = math.gcd(T, HG_CHUNK)
    n = T // c

    def chunks(a):
        return jnp.transpose(a.astype(jnp.float32).reshape(B, n, c, H, a.shape[-1]), (1, 0, 3, 2, 4))

    causal = jnp.tril(jnp.ones((c, c), dtype=bool))[:, :, None]

    def step(s, inp):
        qi, ki, vi, gi = inp
        b = jnp.cumsum(gi, axis=2)
        diff = b[:, :, :, None, :] - b[:, :, None, :, :]
        decay = jnp.where(causal, jnp.exp(jnp.where(causal, diff, 0.0)), 0.0)
        scores = jnp.einsum("bhtk,bhsk,bhtsk->bhts", qi, ki, decay)
        o = (jnp.einsum("bhts,bhsv->bhtv", scores, vi)
             + jnp.einsum("bhtk,bhkv->bhtv", qi * jnp.exp(b), s))
        b_last = b[:, :, -1:, :]
        s = (jnp.exp(b_last[:, :, 0, :])[..., None] * s
             + jnp.einsum("bhsk,bhsv->bhkv", ki * jnp.exp(b_last - b), vi))
        return s, o

    s_final, o = lax.scan(step, s0.astype(jnp.float32),
                          (chunks(q), chunks(k), chunks(v), chunks(log_f)))
    return jnp.transpose(o, (1, 0, 3, 2, 4)).reshape(B, T, H, V), s_final


def _conv_ffn(h, prev, w_up, conv_w, conv_b, w_down):
    T = h.shape[1]
    a, gate = jnp.split(h @ w_up, 2, axis=-1)
    a_ext = jnp.concatenate([prev.astype(a.dtype), a], axis=1)
    conv = conv_b + sum(conv_w[j] * a_ext[:, j:j + T] for j in range(CONV_WIDTH))
    y = (jax.nn.gelu(conv, approximate=False) * gate) @ w_down
    return y, a_ext[:, T:]


def _layer(x, c, past_k, past_v, past_logf, hg_s0, conv_prev, q_offset, lb,
           w_in, fox_f_bias, hg_norm_g, w_br_a, w_br_b, w_o, ln1_g, ln1_b,
           w_up, conv_w, conv_b, w_down, ln2_g, ln2_b, w_ada, b_ada):
    B, T, _ = x.shape
    f32 = jnp.float32
    ada = (jax.nn.silu(c) @ w_ada + b_ada)[:, None, :]
    sh_m, sc_m, g_m, sh_f, sc_f, g_f = jnp.split(ada, 6, axis=-1)

    h = x * (1 + sc_m) + sh_m
    proj = h @ w_in
    q_a, k_a, v_a, f_a, q_b, f_b, i_b, og_b, gt_a, gt_b = jnp.split(proj, _split_points(), axis=-1)

    q_a = q_a.reshape(B, T, FOX_HEADS, FOX_HEAD_DIM)
    k_a = k_a.reshape(B, T, FOX_HEADS, FOX_HEAD_DIM)
    v_a = v_a.reshape(B, T, FOX_HEADS, FOX_HEAD_DIM)
    logf = jax.nn.log_sigmoid((f_a + fox_f_bias).astype(f32))
    if past_k is None:
        k_all, v_all, logf_all = k_a, v_a, logf
    else:
        k_all = jnp.concatenate([past_k.astype(k_a.dtype), k_a], axis=1)
        v_all = jnp.concatenate([past_v.astype(v_a.dtype), v_a], axis=1)
        logf_all = jnp.concatenate([past_logf.astype(f32), logf], axis=1)
    d_all = jnp.cumsum(logf_all, axis=1)
    attn = _fox_attention(q_a, k_all, v_all, d_all[:, -T:], d_all, q_offset)
    y_a = attn.reshape(B, T, FOX_WIDTH) @ w_br_a

    g_b = lb + (1.0 - lb) * jax.nn.sigmoid(f_b.astype(f32))
    log_g = jnp.log(g_b)
    q_h = jax.nn.silu(q_b).reshape(B, T, HG_HEADS, HG_KEY_DIM)
    k_h = (1.0 - g_b).reshape(B, T, HG_HEADS, HG_KEY_DIM)
    v_h = i_b.reshape(B, T, HG_HEADS, HG_VAL_DIM)
    o_h, s_new = _hgrn2_chunked(q_h, k_h, v_h, log_g.reshape(B, T, HG_HEADS, HG_KEY_DIM), hg_s0)
    o_h = o_h * lax.rsqrt(jnp.mean(jnp.square(o_h), axis=-1, keepdims=True) + RMS_EPS)
    o_h = o_h * hg_norm_g.reshape(HG_HEADS, HG_VAL_DIM)
    y_b = (o_h.reshape(B, T, HG_VWIDTH).astype(x.dtype) * jax.nn.silu(og_b)) @ w_br_b

    merged = jax.nn.sigmoid(gt_a) * y_a + jax.nn.sigmoid(gt_b) * y_b
    x = _layer_norm(ALPHA * x + (1 + g_m) * (merged @ w_o), ln1_g, ln1_b)

    h2 = x * (1 + sc_f) + sh_f
    y2, conv_new = _conv_ffn(h2, conv_prev, w_up, conv_w, conv_b, w_down)
    x = _layer_norm(ALPHA * x + (1 + g_f) * y2, ln2_g, ln2_b)
    return x, k_a, v_a, logf, s_new, conv_new


def setup_inputs(seed: int = 0) -> dict:
    key = jax.random.key(seed)
    ks = jax.random.split(key, 32)
    f32 = jnp.float32
    n_pages = PAST_LEN // PAGE_SIZE
    n_used = DEC_BATCH * n_pages
    n_pool = (n_used * POOL_NUM) // POOL_DEN

    def nrm(k, shape, scale):
        return jax.random.normal(k, shape, f32) * scale

    col_scale = jnp.concatenate([
        jnp.ones((2 * FOX_WIDTH,), f32), jnp.full((FOX_WIDTH,), BETA, f32),
        jnp.ones((FOX_HEADS + 2 * HG_WIDTH,), f32), jnp.full((HG_VWIDTH,), BETA, f32),
        jnp.ones((HG_VWIDTH + 2 * D_MODEL,), f32)])
    perm = jax.random.permutation(ks[7], n_pool)
    return {
        "x_prompt": nrm(ks[0], (BATCH, SEQ, D_MODEL), 1.0),
        "x_sample": nrm(ks[1], (DEC_BATCH, DEC_SEQ, D_MODEL), 1.0),
        "cache_k": nrm(ks[2], (DEPTH, n_pool, PAGE_SIZE, FOX_HEADS, FOX_HEAD_DIM), 1.0),
        "cache_v": nrm(ks[3], (DEPTH, n_pool, PAGE_SIZE, FOX_HEADS, FOX_HEAD_DIM), 1.0),
        "cache_logf": jax.nn.log_sigmoid(FOX_GATE_BIAS_MEAN + nrm(ks[4], (DEPTH, n_pool, PAGE_SIZE, FOX_HEADS), 1.0)),
        "state_hgrn": nrm(ks[5], (DEPTH, DEC_BATCH, HG_HEADS, HG_KEY_DIM, HG_VAL_DIM), 0.5),
        "state_conv": nrm(ks[6], (DEPTH, DEC_BATCH, CONV_WIDTH - 1, D_FF), 1.0),
        "page_table": perm[:n_used].reshape(DEC_BATCH, n_pages).astype(jnp.int32),
        "c_prompt": nrm(ks[8], (BATCH, D_MODEL), 1.0),
        "c_sample": nrm(ks[9], (DEC_BATCH, D_MODEL), 1.0),
        "ln0_g": 1.0 + nrm(ks[10], (D_MODEL,), 0.02),
        "ln0_b": nrm(ks[11], (D_MODEL,), 0.02),
        "w_in": nrm(ks[12], (DEPTH, D_MODEL, IN_WIDTH), D_MODEL ** -0.5) * col_scale,
        "fox_f_bias": FOX_GATE_BIAS_MEAN + nrm(ks[13], (DEPTH, FOX_HEADS), 0.5),
        "hg_lb": nrm(ks[14], (DEPTH, HG_WIDTH), 1.0),
        "hg_norm_g": 1.0 + nrm(ks[15], (DEPTH, HG_VWIDTH), 0.02),
        "w_br_a": nrm(ks[16], (DEPTH, FOX_WIDTH, D_MODEL), BETA * FOX_WIDTH ** -0.5),
        "w_br_b": nrm(ks[17], (DEPTH, HG_VWIDTH, D_MODEL), BETA * HG_VWIDTH ** -0.5),
        "w_o": nrm(ks[18], (DEPTH, D_MODEL, D_MODEL), BETA * D_MODEL ** -0.5),
        "ln1_g": 1.0 + nrm(ks[19], (DEPTH, D_MODEL), 0.02),
        "ln1_b": nrm(ks[20], (DEPTH, D_MODEL), 0.02),
        "w_up": nrm(ks[21], (DEPTH, D_MODEL, 2 * D_FF), BETA * D_MODEL ** -0.5),
        "conv_w": nrm(ks[22], (DEPTH, CONV_WIDTH, D_FF), CONV_WIDTH ** -0.5),
        "conv_b": nrm(ks[23], (DEPTH, D_FF), 0.02),
        "w_down": nrm(ks[24], (DEPTH, D_FF, D_MODEL), BETA * D_FF ** -0.5),
        "ln2_g": 1.0 + nrm(ks[25], (DEPTH, D_MODEL), 0.02),
        "ln2_b": nrm(ks[26], (DEPTH, D_MODEL), 0.02),
        "w_ada": nrm(ks[27], (DEPTH, D_MODEL, 6 * D_MODEL), 0.2 * D_MODEL ** -0.5),
        "b_ada": nrm(ks[28], (DEPTH, 6 * D_MODEL), 0.02),
    }


def reference(x_prompt, x_sample, cache_k, cache_v, cache_logf, state_hgrn, state_conv, page_table,
              c_prompt, c_sample, ln0_g, ln0_b, w_in, fox_f_bias, hg_lb, hg_norm_g, w_br_a, w_br_b,
              w_o, ln1_g, ln1_b, w_up, conv_w, conv_b, w_down, ln2_g, ln2_b, w_ada, b_ada):
    n_pages = PAST_LEN // PAGE_SIZE
    past_len = n_pages * PAGE_SIZE
    nb_p = x_prompt.shape[0]
    nb_s = page_table.shape[0]
    lb_soft = jax.nn.softmax(hg_lb.astype(jnp.float32), axis=0)
    lb_all = jnp.cumsum(lb_soft, axis=0) - lb_soft[0]

    xp = _layer_norm(x_prompt, ln0_g, ln0_b)
    xs = _layer_norm(x_sample, ln0_g, ln0_b)
    s0_p = jnp.zeros((nb_p, HG_HEADS, HG_KEY_DIM, HG_VAL_DIM), jnp.float32)
    conv0_p = jnp.zeros((nb_p, CONV_WIDTH - 1, D_FF), x_prompt.dtype)

    new_p = []
    new_s = []
    for l in range(DEPTH):
        w = (lb_all[l], w_in[l], fox_f_bias[l], hg_norm_g[l], w_br_a[l], w_br_b[l], w_o[l],
             ln1_g[l], ln1_b[l], w_up[l], conv_w[l], conv_b[l], w_down[l], ln2_g[l], ln2_b[l],
             w_ada[l], b_ada[l])
        xp, *st_p = _layer(xp, c_prompt, None, None, None, s0_p, conv0_p, 0, *w)
        past_k = cache_k[l][page_table].reshape(nb_s, past_len, FOX_HEADS, FOX_HEAD_DIM)
        past_v = cache_v[l][page_table].reshape(nb_s, past_len, FOX_HEADS, FOX_HEAD_DIM)
        past_logf = cache_logf[l][page_table].reshape(nb_s, past_len, FOX_HEADS)
        xs, *st_s = _layer(xs, c_sample, past_k, past_v, past_logf, state_hgrn[l], state_conv[l],
                           past_len, *w)
        new_p.append(st_p)
        new_s.append(st_s)

    k_prompt = jnp.stack([s[0] for s in new_p])
    v_prompt = jnp.stack([s[1] for s in new_p])
    logf_prompt = jnp.stack([s[2] for s in new_p])
    hgrn_prompt = jnp.stack([s[3] for s in new_p])
    conv_prompt = jnp.stack([s[4] for s in new_p])
    k_sample = jnp.stack([s[0] for s in new_s])
    v_sample = jnp.stack([s[1] for s in new_s])
    logf_sample = jnp.stack([s[2] for s in new_s])
    hgrn_sample = jnp.stack([s[3] for s in new_s])
    conv_sample = jnp.stack([s[4] for s in new_s])
    return (xp, xs, k_prompt, v_prompt, logf_prompt, hgrn_prompt, conv_prompt,
            k_sample, v_sample, logf_sample, hgrn_sample, conv_sample)
```

```python
import functools

import numpy as np
import jax
import jax.numpy as jnp
from jax import lax
from jax.experimental import pallas as pl
from jax.experimental.pallas import tpu as pltpu

F32 = jnp.float32
BF16 = jnp.bfloat16

LN_EPS = 1e-5
RMS_EPS = 1e-6
MASK_VALUE = -1e30
LANES = 128
SUBLANES = 8
VMEM_LIMIT = 56 * 1024 * 1024

NT_DIMS = (((1,), (1,)), ((), ()))
TN_DIMS = (((0,), (0,)), ((), ()))


def _params(sem):
    return pltpu.CompilerParams(dimension_semantics=sem, vmem_limit_bytes=VMEM_LIMIT)


def _largest_divisor(n, target, mult):
    best = None
    for d in range(1, n + 1):
        if n % d == 0 and d <= target and d % mult == 0:
            best = d
    return n if best is None else best


def _row_tiles(nb, g, target):
    if g >= target:
        return 1, _largest_divisor(g, target, SUBLANES)
    return _largest_divisor(nb, max(1, target // g), 1), g


def _split3(x):
    hi = x.astype(BF16)
    r1 = x - hi.astype(F32)
    mid = r1.astype(BF16)
    lo = (r1 - mid.astype(F32)).astype(BF16)
    return hi, mid, lo


def _dot(a, b):
    return jnp.dot(a, b, preferred_element_type=F32)


def _dot3(a_bf16, x_f32):
    hi, mid, lo = _split3(x_f32)
    return _dot(a_bf16, hi) + _dot(a_bf16, mid) + _dot(a_bf16, lo)


def _layer_norm(x, g, b):
    mu = jnp.mean(x, axis=-1, keepdims=True)
    xc = x - mu
    var = jnp.mean(xc * xc, axis=-1, keepdims=True)
    return xc * lax.rsqrt(var + LN_EPS) * g + b


def _sigmoid(x):
    return 1.0 / (1.0 + jnp.exp(-x))


def _silu(x):
    return x * _sigmoid(x)


def _log_sigmoid(x):
    return jnp.minimum(x, 0.0) - jnp.log1p(jnp.exp(-jnp.abs(x)))


def _tri_lower(n):
    r = lax.broadcasted_iota(jnp.int32, (n, n), 0)
    c = lax.broadcasted_iota(jnp.int32, (n, n), 1)
    return jnp.where(r >= c, 1.0, 0.0).astype(BF16)


def _ln0_kernel(x_ref, g_ref, b_ref, o_ref):
    o_ref[...] = _layer_norm(x_ref[...], g_ref[...], b_ref[...])


def _ln0(x2, g, b):
    n, d = x2.shape
    tm = _largest_divisor(n, 1024, SUBLANES)
    return pl.pallas_call(
        _ln0_kernel,
        grid=(n // tm,),
        in_specs=[pl.BlockSpec((tm, d), lambda i: (i, 0)),
                  pl.BlockSpec((1, d), lambda i: (0, 0)),
                  pl.BlockSpec((1, d), lambda i: (0, 0))],
        out_specs=pl.BlockSpec((tm, d), lambda i: (i, 0)),
        out_shape=jax.ShapeDtypeStruct((n, d), F32),
        compiler_params=_params(("parallel",)),
        name="ln0",
    )(x2, g.reshape(1, d), b.reshape(1, d))


def _ada_kernel(c_ref, w_ref, b_ref, o_ref):
    s = _silu(c_ref[...]).astype(BF16)
    o_ref[0] = _dot(s, w_ref[0].astype(BF16)) + b_ref[0]


def _ada(c_all, w_ada, b_ada):
    depth, d, n6 = w_ada.shape
    nc = c_all.shape[0]
    tn = _largest_divisor(n6, 1536, LANES)
    return pl.pallas_call(
        _ada_kernel,
        grid=(depth, n6 // tn),
        in_specs=[pl.BlockSpec((nc, d), lambda l, j: (0, 0)),
                  pl.BlockSpec((1, d, tn), lambda l, j: (l, 0, j)),
                  pl.BlockSpec((1, 1, tn), lambda l, j: (l, 0, j))],
        out_specs=pl.BlockSpec((1, nc, tn), lambda l, j: (l, 0, j)),
        out_shape=jax.ShapeDtypeStruct((depth, nc, n6), F32),
        compiler_params=_params(("parallel", "parallel")),
        name="ada",
    )(c_all, w_ada, b_ada.reshape(depth, 1, n6))


def _inproj_kernel(x_ref, sh_ref, sc_ref, w_ref, wf_ref, fb_ref, proj_ref, logf_ref, h_scr):
    tb, tg, d = x_ref.shape

    @pl.when(pl.program_id(2) == 0)
    def _():
        h = x_ref[...] * (1.0 + sc_ref[...]) + sh_ref[...]
        h_scr[...] = h.reshape(tb * tg, d).astype(BF16)
        f = _dot(h_scr[...], wf_ref[...]) + fb_ref[...]
        logf_ref[...] = _log_sigmoid(f).reshape(tb, tg, LANES)

    proj_ref[...] = _dot(h_scr[...], w_ref[...]).reshape(tb, tg, -1)


def _inproj(x3, ada3, w_main, w_f, f_bias, tn=512, rows=1024):
    nb, g, d = x3.shape
    width = w_main.shape[1]
    tb, tg = _row_tiles(nb, g, rows)
    return pl.pallas_call(
        _inproj_kernel,
        grid=(nb // tb, g // tg, width // tn),
        in_specs=[pl.BlockSpec((tb, tg, d), lambda i, t, j: (i, t, 0)),
                  pl.BlockSpec((tb, 1, d), lambda i, t, j: (i, 0, 0)),
                  pl.BlockSpec((tb, 1, d), lambda i, t, j: (i, 0, 1)),
                  pl.BlockSpec((d, tn), lambda i, t, j: (0, j)),
                  pl.BlockSpec((d, LANES), lambda i, t, j: (0, 0)),
                  pl.BlockSpec((1, LANES), lambda i, t, j: (0, 0))],
        out_specs=[pl.BlockSpec((tb, tg, tn), lambda i, t, j: (i, t, j)),
                   pl.BlockSpec((tb, tg, LANES), lambda i, t, j: (i, t, 0))],
        out_shape=[jax.ShapeDtypeStruct((nb, g, width), F32),
                   jax.ShapeDtypeStruct((nb, g, LANES), F32)],
        scratch_shapes=[pltpu.VMEM((tb * tg, d), BF16)],
        compiler_params=_params(("parallel", "parallel", "arbitrary")),
        name="inproj",
    )(x3, ada3, ada3, w_main, w_f, f_bias)


def _aug_constants(heads, dh):
    width = heads * LANES
    pq = np.zeros((3, LANES, width), np.float32)
    pk = np.zeros((3, LANES, width), np.float32)
    ones_q = np.zeros((1, width), np.float32)
    ones_k = np.zeros((1, width), np.float32)
    ones_v = np.zeros((1, width), np.float32)
    own = np.zeros((1, width), np.float32)
    for h in range(heads):
        o_h = (h % 2) * dh
        a_h = h * LANES + (dh - o_h)
        own[0, h * LANES + o_h:h * LANES + o_h + dh] = 1.0
        for i in range(3):
            pq[i, h, a_h + i] = 1.0
            pk[i, h, a_h + 3 + i] = 1.0
            ones_q[0, a_h + 3 + i] = 1.0
            ones_k[0, a_h + i] = 1.0
        ones_v[0, a_h] = 1.0
    return (jnp.asarray(pq, BF16), jnp.asarray(pk, BF16), jnp.asarray(ones_q), jnp.asarray(ones_k),
            jnp.asarray(ones_v), jnp.asarray(own))


def _prep_kernel(q_ref, k_ref, v_ref, lf_ref, pq_ref, pk_ref, oq_ref, ok_ref, ov_ref, own_ref,
                 qa_ref, ka_ref, va_ref, carry, *, scale):
    tc = q_ref.shape[1]
    heads = qa_ref.shape[2] // LANES

    @pl.when(pl.program_id(1) == 0)
    def _():
        carry[...] = jnp.zeros_like(carry)

    d = _dot3(_tri_lower(tc), lf_ref[0]) + carry[...]
    carry[...] = d[tc - 1:tc, :]
    hi, mid, lo = _split3(d)
    aux_q = _dot(hi, pq_ref[0]) + _dot(mid, pq_ref[1]) + _dot(lo, pq_ref[2]) + oq_ref[...]
    aux_k = ok_ref[...] - (_dot(hi, pk_ref[0]) + _dot(mid, pk_ref[1]) + _dot(lo, pk_ref[2]))
    own = own_ref[...] > 0.5

    def widen(x):
        return jnp.concatenate([x[:, (h // 2) * LANES:(h // 2 + 1) * LANES] for h in range(heads)], axis=1)

    qa_ref[0] = jnp.where(own, widen(q_ref[0]) * scale, aux_q).astype(BF16)
    ka_ref[0] = jnp.where(own, widen(k_ref[0]), aux_k).astype(BF16)
    va_ref[0] = jnp.where(own, widen(v_ref[0]), ov_ref[...]).astype(BF16)


def _attn_prep(proj, logf, consts, cols, heads, dh):
    b, t, _ = proj.shape
    hw = heads * dh
    width = heads * LANES
    tc = _largest_divisor(t, 512, SUBLANES)
    pq, pk, oq, ok, ov, own = consts
    cq, ck, cv = cols
    full = lambda shape: pl.BlockSpec(shape, lambda i, j: (0,) * len(shape))
    out = jax.ShapeDtypeStruct((b, t, width), BF16)
    return pl.pallas_call(
        functools.partial(_prep_kernel, scale=dh ** -0.5),
        grid=(b, t // tc),
        in_specs=[pl.BlockSpec((1, tc, hw), lambda i, j: (i, j, cq)),
                  pl.BlockSpec((1, tc, hw), lambda i, j: (i, j, ck)),
                  pl.BlockSpec((1, tc, hw), lambda i, j: (i, j, cv)),
                  pl.BlockSpec((1, tc, LANES), lambda i, j: (i, j, 0)),
                  full(pq.shape), full(pk.shape), full(oq.shape), full(ok.shape), full(ov.shape),
                  full(own.shape)],
        out_specs=[pl.BlockSpec((1, tc, width), lambda i, j: (i, j, 0))] * 3,
        out_shape=[out, out, out],
        scratch_shapes=[pltpu.VMEM((1, LANES), F32)],
        compiler_params=_params(("parallel", "arbitrary")),
        name="attn_prep",
    )(proj, proj, proj, logf, pq, pk, oq, ok, ov, own)


def _attn_kernel(qi_ref, kj_ref, q_ref, k_ref, v_ref, o_ref, m_scr, acc_scr, *, dh):
    tq = q_ref.shape[1]
    tk = k_ref.shape[1]
    heads = q_ref.shape[2] // LANES
    step = pl.program_id(1)
    i = qi_ref[step]
    j = kj_ref[step]

    @pl.when(j == 0)
    def _():
        m_scr[...] = jnp.full_like(m_scr, MASK_VALUE)
        acc_scr[...] = jnp.zeros_like(acc_scr)

    def update(masked):
        if masked:
            row = lax.broadcasted_iota(jnp.int32, (tq, tk), 0)
            col = lax.broadcasted_iota(jnp.int32, (tq, tk), 1)
            keep = col <= row
        for h in range(heads):
            blk = slice(h * LANES, (h + 1) * LANES)
            s = lax.dot_general(q_ref[0, :, blk], k_ref[0, :, blk], NT_DIMS, preferred_element_type=F32)
            if masked:
                s = jnp.where(keep, s, MASK_VALUE)
            m_prev = m_scr[h]
            m_new = jnp.maximum(m_prev, jnp.max(s, axis=-1, keepdims=True))
            p = jnp.exp(s - m_new).astype(BF16)
            acc_scr[h] = jnp.exp(m_prev - m_new) * acc_scr[h] + _dot(p, v_ref[0, :, blk])
            m_scr[h] = m_new

    @pl.when(j < i)
    def _():
        update(False)

    @pl.when(j == i)
    def _():
        update(True)
        lane = lax.broadcasted_iota(jnp.int32, (tq, LANES), 1)
        for hp in range(heads // 2):
            a0 = acc_scr[2 * hp]
            a1 = acc_scr[2 * hp + 1]
            o = jnp.where(lane < dh, a0 / a0[:, dh:dh + 1], a1 / a1[:, 0:1])
            o_ref[0, :, hp * LANES:(hp + 1) * LANES] = o.astype(o_ref.dtype)


def _attn_prompt(qa, ka, va, heads, dh):
    b, t, width = qa.shape
    tq = _largest_divisor(t, 512, SUBLANES)
    nq = t // tq
    qi = np.array([i for i in range(nq) for j in range(i + 1)], np.int32)
    kj = np.array([j for i in range(nq) for j in range(i + 1)], np.int32)
    grid_spec = pltpu.PrefetchScalarGridSpec(
        num_scalar_prefetch=2,
        grid=(b, len(qi)),
        in_specs=[pl.BlockSpec((1, tq, width), lambda bb, s, qi_r, kj_r: (bb, qi_r[s], 0)),
                  pl.BlockSpec((1, tq, width), lambda bb, s, qi_r, kj_r: (bb, kj_r[s], 0)),
                  pl.BlockSpec((1, tq, width), lambda bb, s, qi_r, kj_r: (bb, kj_r[s], 0))],
        out_specs=pl.BlockSpec((1, tq, heads * dh), lambda bb, s, qi_r, kj_r: (bb, qi_r[s], 0)),
        scratch_shapes=[pltpu.VMEM((heads, tq, 1), F32), pltpu.VMEM((heads, tq, LANES), F32)],
    )
    return pl.pallas_call(
        functools.partial(_attn_kernel, dh=dh),
        grid_spec=grid_spec,
        out_shape=jax.ShapeDtypeStruct((b, t, heads * dh), BF16),
        compiler_params=_params(("parallel", "arbitrary")),
        name="attn_prompt",
    )(jnp.asarray(qi), jnp.asarray(kj), qa, ka, va)


def _attn_sample_kernel(pt_ref, q_ref, kc_ref, vc_ref, lfc_ref, ck_hbm, cv_hbm, clf_hbm, o_ref,
                        kbuf, vbuf, lbuf, sems, *, layer, n_pages, heads, dh):
    b = pl.program_id(0)
    nb = pl.num_programs(0)
    page = lbuf.shape[2]
    hw = heads * dh
    t_cur = q_ref.shape[1]
    rows = heads * t_cur
    npr = n_pages * heads

    def copies(seq, slot):
        out = []
        for p in range(n_pages):
            pid = pt_ref[seq, p]
            lanes = pl.ds(p * page, page)
            out.append(pltpu.make_async_copy(ck_hbm.at[layer, pid], kbuf.at[slot, :, lanes], sems.at[0, slot, p]))
            out.append(pltpu.make_async_copy(cv_hbm.at[layer, pid], vbuf.at[slot, :, lanes], sems.at[1, slot, p]))
            out.append(pltpu.make_async_copy(clf_hbm.at[layer, pid], lbuf.at[slot, pl.ds(p * heads, heads), :],
                                             sems.at[2, slot, p]))
        return out

    @pl.when(b == 0)
    def _():
        for c in copies(0, 0):
            c.start()

    slot = b % 2

    @pl.when(b + 1 < nb)
    def _():
        for c in copies(b + 1, 1 - slot):
            c.start()

    for c in copies(b, slot):
        c.wait()

    ur = lax.broadcasted_iota(jnp.int32, (page, page), 0)
    uc = lax.broadcasted_iota(jnp.int32, (page, page), 1)
    triu = jnp.where(ur <= uc, 1.0, 0.0).astype(BF16)
    l_hi, l_mid, l_lo = _split3(lbuf[slot])
    cum = _dot(l_hi, triu) + _dot(l_mid, triu) + _dot(l_lo, triu)
    mr = lax.broadcasted_iota(jnp.int32, (npr, npr), 0)
    mc = lax.broadcasted_iota(jnp.int32, (npr, npr), 1)
    later = jnp.where((mr % heads == mc % heads) & (mc // heads >= mr // heads), 1.0, 0.0).astype(BF16)
    e = cum - _dot3(later, jnp.broadcast_to(cum[:, page - 1:page], (npr, page)))
    rr = lax.broadcasted_iota(jnp.int32, (n_pages * rows, npr), 0)
    rc = lax.broadcasted_iota(jnp.int32, (n_pages * rows, npr), 1)
    spread = jnp.where((rr // rows == rc // heads) & ((rr % rows) // t_cur == rc % heads), -1.0, 0.0).astype(BF16)
    bias_rows = _dot3(spread, e)
    bias = jnp.concatenate([bias_rows[p * rows:(p + 1) * rows] for p in range(n_pages)], axis=1)

    cc = _dot3(_tri_lower(t_cur), lfc_ref[0])
    rrow = lax.broadcasted_iota(jnp.int32, (rows, hw), 0) // t_cur
    rcol = lax.broadcasted_iota(jnp.int32, (rows, hw), 1) // dh
    q_rep = jnp.concatenate([q_ref[0]] * heads, axis=0)
    qbd = jnp.where(rrow == rcol, q_rep * (dh ** -0.5), 0.0).astype(BF16)
    arow = lax.broadcasted_iota(jnp.int32, (rows, LANES), 0) // t_cur
    acol = lax.broadcasted_iota(jnp.int32, (rows, LANES), 1)
    sel = arow == acol
    neg_sel = jnp.where(sel, -1.0, 0.0).astype(BF16)
    cc_rep = jnp.concatenate([cc] * heads, axis=0)
    dq = jnp.sum(jnp.where(sel, cc_rep, 0.0), axis=-1, keepdims=True)

    nt = lambda a, bm: lax.dot_general(a, bm, NT_DIMS, preferred_element_type=F32)
    s_past = _dot(qbd, kbuf[slot].astype(BF16)) + bias + dq
    c_hi, c_mid, c_lo = _split3(cc)
    s_cur = nt(qbd, kc_ref[0].astype(BF16)) + (nt(neg_sel, c_hi) + nt(neg_sel, c_mid) + nt(neg_sel, c_lo)) + dq
    tq_idx = lax.broadcasted_iota(jnp.int32, (rows, t_cur), 0) % t_cur
    ts_idx = lax.broadcasted_iota(jnp.int32, (rows, t_cur), 1)
    s_cur = jnp.where(ts_idx <= tq_idx, s_cur, MASK_VALUE)

    m = jnp.maximum(jnp.max(s_past, axis=-1, keepdims=True), jnp.max(s_cur, axis=-1, keepdims=True))
    p_past = jnp.exp(s_past - m)
    p_cur = jnp.exp(s_cur - m)
    denom = jnp.sum(p_past, axis=-1, keepdims=True) + jnp.sum(p_cur, axis=-1, keepdims=True)
    o_all = (nt(p_past.astype(BF16), vbuf[slot].astype(BF16))
             + _dot(p_cur.astype(BF16), vc_ref[0].astype(BF16))) / denom
    o_all = jnp.where(rrow == rcol, o_all, 0.0)
    o = o_all[0:t_cur]
    for h in range(1, heads):
        o = o + o_all[h * t_cur:(h + 1) * t_cur]
    o_ref[0] = o.astype(o_ref.dtype)


def _attn_sample(proj, logf, cache_k, cache_v, cache_logf, page_table, layer, cols, heads, dh):
    nb, t_cur, _ = proj.shape
    hw = heads * dh
    depth, n_pool, page = cache_k.shape[:3]
    n_pages = page_table.shape[1]
    ck = jnp.transpose(cache_k, (0, 1, 3, 4, 2)).reshape(depth, n_pool, hw, page)
    cv = jnp.transpose(cache_v, (0, 1, 3, 4, 2)).reshape(depth, n_pool, hw, page)
    clf = jnp.transpose(cache_logf, (0, 1, 3, 2))
    cq, ckc, cvc = cols
    grid_spec = pltpu.PrefetchScalarGridSpec(
        num_scalar_prefetch=1,
        grid=(nb,),
        in_specs=[pl.BlockSpec((1, t_cur, hw), lambda b, pt: (b, 0, cq)),
                  pl.BlockSpec((1, t_cur, hw), lambda b, pt: (b, 0, ckc)),
                  pl.BlockSpec((1, t_cur, hw), lambda b, pt: (b, 0, cvc)),
                  pl.BlockSpec((1, t_cur, LANES), lambda b, pt: (b, 0, 0)),
                  pl.BlockSpec(memory_space=pl.ANY),
                  pl.BlockSpec(memory_space=pl.ANY),
                  pl.BlockSpec(memory_space=pl.ANY)],
        out_specs=pl.BlockSpec((1, t_cur, hw), lambda b, pt: (b, 0, 0)),
        scratch_shapes=[pltpu.VMEM((2, hw, n_pages * page), F32),
                        pltpu.VMEM((2, hw, n_pages * page), F32),
                        pltpu.VMEM((2, n_pages * heads, page), F32),
                        pltpu.SemaphoreType.DMA((3, 2, n_pages))],
    )
    return pl.pallas_call(
        functools.partial(_attn_sample_kernel, layer=layer, n_pages=n_pages, heads=heads, dh=dh),
        grid_spec=grid_spec,
        out_shape=jax.ShapeDtypeStruct((nb, t_cur, hw), BF16),
        compiler_params=_params(("arbitrary",)),
        name="attn_sample",
    )(page_table, proj, proj, proj, logf, ck, cv, clf)


def _hgrn_kernel(q_ref, f_ref, v_ref, og_ref, lb_ref, ng_ref, s0_ref, y_ref, sout_ref, st_scr, *, chunk, sub):
    tb, tt, kd = q_ref.shape
    n_chunks = tt // chunk
    n_sub = chunk // sub
    t_idx = pl.program_id(2)

    @pl.when(t_idx == 0)
    def _():
        for ib in range(tb):
            st_scr[ib] = s0_ref[ib, 0].T

    lb = lb_ref[...]
    ng = ng_ref[...]
    tri = _tri_lower(chunk)
    sub_row = lax.broadcasted_iota(jnp.int32, (sub, kd), 0)

    def do_chunk(ib, rows):
        g = lb + (1.0 - lb) * _sigmoid(f_ref[ib, rows, :])
        kk = 1.0 - g
        qq = _silu(q_ref[ib, rows, :])
        v = v_ref[ib, rows, :]
        vb = v.astype(BF16)
        bcum = _dot3(tri, jnp.log(g))
        st = st_scr[ib]
        o = lax.dot_general((qq * jnp.exp(bcum)).astype(BF16), st.astype(BF16), NT_DIMS,
                            preferred_element_type=F32)
        pieces = []
        for si in range(n_sub):
            lo_r, hi_r = si * sub, (si + 1) * sub
            bs, qs, ks, vs = bcum[lo_r:hi_r], qq[lo_r:hi_r], kk[lo_r:hi_r], v[lo_r:hi_r]
            acc = o[lo_r:hi_r]
            if si > 0:
                ref = bcum[lo_r - 1:lo_r]
                a = (qs * jnp.exp(bs - ref)).astype(BF16)
                bm = (kk[:lo_r] * jnp.exp(ref - bcum[:lo_r])).astype(BF16)
                sc = lax.dot_general(a, bm, NT_DIMS, preferred_element_type=F32)
                acc = acc + _dot(sc.astype(BF16), vb[:lo_r])
            for s in range(sub):
                keep = sub_row >= s
                dec = jnp.exp(jnp.where(keep, bs - bs[s:s + 1], 0.0))
                w = jnp.sum(jnp.where(keep, qs * ks[s:s + 1] * dec, 0.0), axis=-1, keepdims=True)
                acc = acc + w * vs[s:s + 1]
            pieces.append(acc)
        o = jnp.concatenate(pieces, axis=0) if n_sub > 1 else pieces[0]
        b_last = bcum[chunk - 1:chunk]
        kd_ = (kk * jnp.exp(b_last - bcum)).astype(BF16)
        st_scr[ib] = jnp.exp(b_last) * st + lax.dot_general(vb, kd_, TN_DIMS, preferred_element_type=F32)
        o = o * lax.rsqrt(jnp.mean(o * o, axis=-1, keepdims=True) + RMS_EPS) * ng
        y_ref[ib, rows, :] = (o * _silu(og_ref[ib, rows, :])).astype(y_ref.dtype)

    for ib in range(tb):
        if n_chunks == 1:
            do_chunk(ib, slice(0, chunk))
        else:
            def body(c, carry, ib=ib):
                do_chunk(ib, pl.ds(pl.multiple_of(c * chunk, chunk), chunk))
                return carry
            lax.fori_loop(0, n_chunks, body, 0)

    @pl.when(t_idx == pl.num_programs(2) - 1)
    def _():
        for ib in range(tb):
            sout_ref[ib, 0] = st_scr[ib].T


def _hgrn(proj, s0, lb, norm_g, cols, hg_heads):
    nb, g, _ = proj.shape
    kd = s0.shape[-2]
    vd = s0.shape[-1]
    cq, cf, cv, cg = cols
    if g >= 64:
        tb, tt = 1, _largest_divisor(g, 512, 64)
        chunk, sub = 64, 16
    else:
        tb, tt = _largest_divisor(nb, 8, 1), g
        chunk, sub = g, g
    col = lambda c0: pl.BlockSpec((tb, tt, kd), lambda i, h, t: (i, t, c0 + h))
    return pl.pallas_call(
        functools.partial(_hgrn_kernel, chunk=chunk, sub=sub),
        grid=(nb // tb, hg_heads, g // tt),
        in_specs=[col(cq), col(cf), col(cv), col(cg),
                  pl.BlockSpec((1, kd), lambda i, h, t: (0, h)),
                  pl.BlockSpec((1, vd), lambda i, h, t: (0, h)),
                  pl.BlockSpec((tb, 1, kd, vd), lambda i, h, t: (i, h, 0, 0))],
        out_specs=[pl.BlockSpec((tb, tt, vd), lambda i, h, t: (i, t, h)),
                   pl.BlockSpec((tb, 1, kd, vd), lambda i, h, t: (i, h, 0, 0))],
        out_shape=[jax.ShapeDtypeStruct((nb, g, hg_heads * vd), BF16),
                   jax.ShapeDtypeStruct(s0.shape, F32)],
        scratch_shapes=[pltpu.VMEM((tb, vd, kd), F32)],
        compiler_params=_params(("parallel", "parallel", "arbitrary")),
        name="hgrn",
    )(proj, proj, proj, proj, lb.reshape(1, -1), norm_g.reshape(1, -1), s0)


def _merge_kernel(ya_ref, yb_ref, ga_ref, gb_ref, x_ref, gm_ref, scf_ref, shf_ref,
                  wa_ref, wb_ref, wo_ref, lg_ref, lbias_ref, x1_ref, h2_ref, *, alpha):
    tb, tg, d = x_ref.shape
    tm = tb * tg
    y_a = _dot(ya_ref[...].reshape(tm, -1), wa_ref[...])
    y_b = _dot(yb_ref[...].reshape(tm, -1), wb_ref[...])
    merged = _sigmoid(ga_ref[...].reshape(tm, d)) * y_a + _sigmoid(gb_ref[...].reshape(tm, d)) * y_b
    z = _dot(merged.astype(BF16), wo_ref[...]).reshape(tb, tg, d)
    x1 = _layer_norm(alpha * x_ref[...] + (1.0 + gm_ref[...]) * z, lg_ref[...], lbias_ref[...])
    x1_ref[...] = x1
    h2_ref[...] = (x1 * (1.0 + scf_ref[...]) + shf_ref[...]).astype(h2_ref.dtype)


def _merge(attn, yb, proj, x3, ada3, w_br_a, w_br_b, w_o, ln_g, ln_b, alpha, gate_cols, rows=512):
    nb, g, d = x3.shape
    tb, tg = _row_tiles(nb, g, rows)
    ca, cb = gate_cols
    wa = attn.shape[-1]
    wb = yb.shape[-1]
    tok = lambda w, c: pl.BlockSpec((tb, tg, w), lambda i, t: (i, t, c))
    mod = lambda c: pl.BlockSpec((tb, 1, d), lambda i, t: (i, 0, c))
    full = lambda shape: pl.BlockSpec(shape, lambda i, t: (0,) * len(shape))
    return pl.pallas_call(
        functools.partial(_merge_kernel, alpha=alpha),
        grid=(nb // tb, g // tg),
        in_specs=[tok(wa, 0), tok(wb, 0), tok(d, ca), tok(d, cb), tok(d, 0),
                  mod(2), mod(4), mod(3),
                  full(w_br_a.shape), full(w_br_b.shape), full(w_o.shape), full((1, d)), full((1, d))],
        out_specs=[tok(d, 0), tok(d, 0)],
        out_shape=[jax.ShapeDtypeStruct((nb, g, d), F32), jax.ShapeDtypeStruct((nb, g, d), BF16)],
        compiler_params=_params(("parallel", "parallel")),
        name="merge",
    )(attn, yb, proj, proj, x3, ada3, ada3, ada3, w_br_a, w_br_b, w_o, ln_g.reshape(1, d), ln_b.reshape(1, d))


def _gelu(x):
    return 0.5 * x * (1.0 + lax.erf(x * (2.0 ** -0.5)))


def _ffn_kernel(h_ref, wa_ref, wg_ref, cw_ref, cb_ref, wd_ref, prev_ref, x1_ref, gf_ref, lg_ref, lbias_ref,
                x2_ref, cnew_ref, acc_scr, carry_scr, *, alpha):
    tb, tg, d = h_ref.shape
    tm = tb * tg
    tf = wa_ref.shape[1]
    t_idx = pl.program_id(1)
    j = pl.program_id(2)
    h = h_ref[...].reshape(tm, d)
    a = _dot(h, wa_ref[...])
    gate = _dot(h, wg_ref[...])
    a3 = a.reshape(tb, tg, tf)

    first = t_idx == 0

    @pl.when(first)
    def _():
        carry_scr[j] = jnp.zeros(carry_scr.shape[1:], F32)

    tail = carry_scr[j]
    p0 = jnp.where(first, prev_ref[:, 0:1, :], tail[:, SUBLANES - 2:SUBLANES - 1, :])
    p1 = jnp.where(first, prev_ref[:, 1:2, :], tail[:, SUBLANES - 1:SUBLANES, :])
    tt = lax.broadcasted_iota(jnp.int32, (tb, tg, tf), 1)
    am1 = jnp.where(tt == 0, p1, pltpu.roll(a, 1, 0).reshape(tb, tg, tf))
    am2 = jnp.where(tt == 0, p0, jnp.where(tt == 1, p1, pltpu.roll(a, 2, 0).reshape(tb, tg, tf)))
    cw = cw_ref[...]
    conv = cb_ref[...] + cw[0:1] * am2 + cw[1:2] * am1 + cw[2:3] * a3
    act = (_gelu(conv) * gate.reshape(tb, tg, tf)).astype(BF16).reshape(tm, tf)
    carry_scr[j] = a3[:, tg - SUBLANES:, :]
    cnew_ref[:, 0] = a3[:, tg - 2:, :]

    @pl.when(j == 0)
    def _():
        acc_scr[...] = jnp.zeros_like(acc_scr)

    acc_scr[...] += _dot(act, wd_ref[...])

    @pl.when(j == pl.num_programs(2) - 1)
    def _():
        y = acc_scr[...].reshape(tb, tg, d)
        x2_ref[...] = _layer_norm(alpha * x1_ref[...] + (1.0 + gf_ref[...]) * y, lg_ref[...], lbias_ref[...])


def _ffn(h2, x1, ada3, prev, w_up, conv_w, conv_b, w_down, ln_g, ln_b, alpha, rows=1024, tf=256):
    nb, g, d = x1.shape
    dff = w_down.shape[0]
    nf = dff // tf
    tb, tg = _row_tiles(nb, g, rows)
    return pl.pallas_call(
        functools.partial(_ffn_kernel, alpha=alpha),
        grid=(nb // tb, g // tg, nf),
        in_specs=[pl.BlockSpec((tb, tg, d), lambda i, t, j: (i, t, 0)),
                  pl.BlockSpec((d, tf), lambda i, t, j: (0, j)),
                  pl.BlockSpec((d, tf), lambda i, t, j: (0, nf + j)),
                  pl.BlockSpec((conv_w.shape[0], tf), lambda i, t, j: (0, j)),
                  pl.BlockSpec((1, tf), lambda i, t, j: (0, j)),
                  pl.BlockSpec((tf, d), lambda i, t, j: (j, 0)),
                  pl.BlockSpec((tb, 2, tf), lambda i, t, j: (i, 0, j)),
                  pl.BlockSpec((tb, tg, d), lambda i, t, j: (i, t, 0)),
                  pl.BlockSpec((tb, 1, d), lambda i, t, j: (i, 0, 5)),
                  pl.BlockSpec((1, d), lambda i, t, j: (0, 0)),
                  pl.BlockSpec((1, d), lambda i, t, j: (0, 0))],
        out_specs=[pl.BlockSpec((tb, tg, d), lambda i, t, j: (i, t, 0)),
                   pl.BlockSpec((tb, 1, 2, tf), lambda i, t, j: (i, t, 0, j))],
        out_shape=[jax.ShapeDtypeStruct((nb, g, d), F32), jax.ShapeDtypeStruct((nb, g // tg, 2, dff), F32)],
        scratch_shapes=[pltpu.VMEM((tb * tg, d), F32), pltpu.VMEM((nf, tb, SUBLANES, tf), F32)],
        compiler_params=_params(("parallel", "arbitrary", "arbitrary")),
        name="ffn",
    )(h2, w_up, w_up, conv_w, conv_b.reshape(1, dff), w_down, prev, x1, ada3,
      ln_g.reshape(1, d), ln_b.reshape(1, d))


def kernel(x_prompt, x_sample, cache_k, cache_v, cache_logf, state_hgrn, state_conv, page_table, c_prompt, c_sample, ln0_g, ln0_b, w_in, fox_f_bias, hg_lb, hg_norm_g, w_br_a, w_br_b, w_o, ln1_g, ln1_b, w_up, conv_w, conv_b, w_down, ln2_g, ln2_b, w_ada, b_ada):
    bp, tp, d = x_prompt.shape
    bs, ts, _ = x_sample.shape
    depth = w_in.shape[0]
    heads, dh = cache_k.shape[3], cache_k.shape[4]
    hw = heads * dh
    hg_heads, kd, vd = state_hgrn.shape[2:]
    hgw = hg_heads * kd
    dff = w_down.shape[1]
    assert conv_w.shape[1] == 3 and state_conv.shape[2] == 2
    assert kd == LANES and vd == LANES and 2 * dh == LANES and hw == hgw == hg_heads * vd and d == 2 * hw
    alpha = (2 * depth) ** 0.25

    lb_soft = jax.nn.softmax(hg_lb.astype(F32), axis=0)
    lb_all = jnp.cumsum(lb_soft, axis=0) - lb_soft[0]

    o_f = 3 * hw
    o_b = o_f + heads
    o_g = o_b + 4 * hgw
    w_main = jnp.concatenate([w_in[:, :, o_g:], w_in[:, :, :o_f], w_in[:, :, o_b:o_g]], axis=2).astype(BF16)
    w_f = jnp.pad(w_in[:, :, o_f:o_b], ((0, 0), (0, 0), (0, LANES - heads))).astype(BF16)
    f_bias = jnp.pad(fox_f_bias, ((0, 0), (0, LANES - heads))).reshape(depth, 1, LANES)
    gate_cols = (0, 1)
    qkv_cols = (4, 5, 6)
    hg_cols = tuple((2 * d + 3 * hw) // LANES + i * hg_heads for i in range(4))
    k_off, v_off = 2 * d + hw, 2 * d + 2 * hw
    w_br_a16, w_br_b16, w_o16 = w_br_a.astype(BF16), w_br_b.astype(BF16), w_o.astype(BF16)
    w_up16, w_down16 = w_up.astype(BF16), w_down.astype(BF16)
    consts = _aug_constants(heads, dh)

    ada = _ada(jnp.concatenate([c_prompt, c_sample], axis=0), w_ada, b_ada)
    x_all = _ln0(jnp.concatenate([x_prompt.reshape(bp * tp, d), x_sample.reshape(bs * ts, d)], axis=0), ln0_g, ln0_b)
    xp = x_all[:bp * tp].reshape(bp, tp, d)
    xs = x_all[bp * tp:].reshape(bs, ts, d)
    s0_p = jnp.zeros((bp,) + state_hgrn.shape[2:], F32)
    conv0_p = jnp.zeros((bp, 2, dff), F32)

    outs_p, outs_s = [], []
    for l in range(depth):
        ada_p = ada[l, :bp].reshape(bp, 1, 6 * d)
        ada_s = ada[l, bp:].reshape(bs, 1, 6 * d)
        new = []
        for grp, (x3, ada3) in enumerate(((xp, ada_p), (xs, ada_s))):
            proj, logf = _inproj(x3, ada3, w_main[l], w_f[l], f_bias[l])
            if grp == 0:
                attn = _attn_prompt(*_attn_prep(proj, logf, consts, qkv_cols, heads, dh), heads, dh)
                s0, prev = s0_p, conv0_p
            else:
                attn = _attn_sample(proj, logf, cache_k, cache_v, cache_logf, page_table, l, qkv_cols, heads, dh)
                s0, prev = state_hgrn[l], state_conv[l]
            yb, s_new = _hgrn(proj, s0, lb_all[l], hg_norm_g[l], hg_cols, hg_heads)
            x1, h2 = _merge(attn, yb, proj, x3, ada3, w_br_a16[l], w_br_b16[l], w_o16[l], ln1_g[l], ln1_b[l],
                            alpha, gate_cols)
            x2, conv_new = _ffn(h2, x1, ada3, prev, w_up16[l], conv_w[l], conv_b[l], w_down16[l],
                                ln2_g[l], ln2_b[l], alpha)
            nbx, gx = x3.shape[:2]
            new.append((x2,
                        proj[:, :, k_off:k_off + hw].reshape(nbx, gx, heads, dh),
                        proj[:, :, v_off:v_off + hw].reshape(nbx, gx, heads, dh),
                        logf[:, :, :heads], s_new, conv_new[:, -1]))
        xp, xs = new[0][0], new[1][0]
        outs_p.append(new[0][1:])
        outs_s.append(new[1][1:])

    stack = lambda outs, i: jnp.stack([o[i] for o in outs])
    return (xp, xs,
            stack(outs_p, 0), stack(outs_p, 1), stack(outs_p, 2), stack(outs_p, 3), stack(outs_p, 4),
            stack(outs_s, 0), stack(outs_s, 1), stack(outs_s, 2), stack(outs_s, 3), stack(outs_s, 4))
```

```python
import functools

import numpy as np
import jax
import jax.numpy as jnp
from jax import lax
from jax.experimental import pallas as pl
from jax.experimental.pallas import tpu as pltpu

F32 = jnp.float32
BF16 = jnp.bfloat16

LN_EPS = 1e-5
RMS_EPS = 1e-6
MASK_VALUE = -1e30
LOG2_E = 1.4426950408889634
LANES = 128
SUBLANES = 8
VMEM_LIMIT = 56 * 1024 * 1024

NT_DIMS = (((1,), (1,)), ((), ()))
TN_DIMS = (((0,), (0,)), ((), ()))


def _params(sem):
    return pltpu.CompilerParams(dimension_semantics=sem, vmem_limit_bytes=VMEM_LIMIT)


def _largest_divisor(n, target, mult):
    best = None
    for d in range(1, n + 1):
        if n % d == 0 and d <= target and d % mult == 0:
            best = d
    return n if best is None else best


def _row_tiles(nb, g, target):
    if g >= target:
        return 1, _largest_divisor(g, target, SUBLANES)
    return _largest_divisor(nb, max(1, target // g), 1), g


def _split3(x):
    hi = x.astype(BF16)
    r1 = x - hi.astype(F32)
    mid = r1.astype(BF16)
    lo = (r1 - mid.astype(F32)).astype(BF16)
    return hi, mid, lo


def _dot(a, b):
    return jnp.dot(a, b, preferred_element_type=F32)


def _dot3(a_bf16, x_f32):
    hi, mid, lo = _split3(x_f32)
    return _dot(a_bf16, hi) + _dot(a_bf16, mid) + _dot(a_bf16, lo)


def _layer_norm(x, g, b):
    mu = jnp.mean(x, axis=-1, keepdims=True)
    xc = x - mu
    var = jnp.mean(xc * xc, axis=-1, keepdims=True)
    return xc * lax.rsqrt(var + LN_EPS) * g + b


def _sigmoid(x):
    return 1.0 / (1.0 + jnp.exp(-x))


def _silu(x):
    return x * _sigmoid(x)


def _log_sigmoid(x):
    return jnp.minimum(x, 0.0) - jnp.log1p(jnp.exp(-jnp.abs(x)))


def _tri_lower(n):
    r = lax.broadcasted_iota(jnp.int32, (n, n), 0)
    c = lax.broadcasted_iota(jnp.int32, (n, n), 1)
    return jnp.where(r >= c, 1.0, 0.0).astype(BF16)


def _ln0_kernel(x_ref, g_ref, b_ref, o_ref):
    o_ref[...] = _layer_norm(x_ref[...], g_ref[...], b_ref[...])


def _ln0(x2, g, b):
    n, d = x2.shape
    tm = _largest_divisor(n, 1024, SUBLANES)
    return pl.pallas_call(
        _ln0_kernel,
        grid=(n // tm,),
        in_specs=[pl.BlockSpec((tm, d), lambda i: (i, 0)),
                  pl.BlockSpec((1, d), lambda i: (0, 0)),
                  pl.BlockSpec((1, d), lambda i: (0, 0))],
        out_specs=pl.BlockSpec((tm, d), lambda i: (i, 0)),
        out_shape=jax.ShapeDtypeStruct((n, d), F32),
        compiler_params=_params(("parallel",)),
        name="ln0",
    )(x2, g.reshape(1, d), b.reshape(1, d))


def _ada_kernel(c_ref, w_ref, b_ref, o_ref):
    s = _silu(c_ref[...]).astype(BF16)
    o_ref[0] = _dot(s, w_ref[0].astype(BF16)) + b_ref[0]


def _ada(c_all, w_ada, b_ada):
    depth, d, n6 = w_ada.shape
    nc = c_all.shape[0]
    tn = _largest_divisor(n6, 1536, LANES)
    return pl.pallas_call(
        _ada_kernel,
        grid=(depth, n6 // tn),
        in_specs=[pl.BlockSpec((nc, d), lambda l, j: (0, 0)),
                  pl.BlockSpec((1, d, tn), lambda l, j: (l, 0, j)),
                  pl.BlockSpec((1, 1, tn), lambda l, j: (l, 0, j))],
        out_specs=pl.BlockSpec((1, nc, tn), lambda l, j: (l, 0, j)),
        out_shape=jax.ShapeDtypeStruct((depth, nc, n6), F32),
        compiler_params=_params(("parallel", "parallel")),
        name="ada",
    )(c_all, w_ada, b_ada.reshape(depth, 1, n6))


def _inproj_kernel(x_ref, sh_ref, sc_ref, w_ref, wf_ref, fb_ref, proj_ref, logf_ref, h_scr):
    tb, tg, d = x_ref.shape

    @pl.when(pl.program_id(2) == 0)
    def _():
        h = x_ref[...] * (1.0 + sc_ref[...]) + sh_ref[...]
        h_scr[...] = h.reshape(tb * tg, d).astype(BF16)
        f = _dot(h_scr[...], wf_ref[...]) + fb_ref[...]
        logf_ref[...] = _log_sigmoid(f).reshape(tb, tg, LANES)

    proj_ref[...] = _dot(h_scr[...], w_ref[...]).reshape(tb, tg, -1)


def _inproj(x3, ada3, w_main, w_f, f_bias, tn=1408, rows=1024):
    nb, g, d = x3.shape
    width = w_main.shape[1]
    tb, tg = _row_tiles(nb, g, rows)
    return pl.pallas_call(
        _inproj_kernel,
        grid=(nb // tb, g // tg, width // tn),
        in_specs=[pl.BlockSpec((tb, tg, d), lambda i, t, j: (i, t, 0)),
                  pl.BlockSpec((tb, 1, d), lambda i, t, j: (i, 0, 0)),
                  pl.BlockSpec((tb, 1, d), lambda i, t, j: (i, 0, 1)),
                  pl.BlockSpec((d, tn), lambda i, t, j: (0, j)),
                  pl.BlockSpec((d, LANES), lambda i, t, j: (0, 0)),
                  pl.BlockSpec((1, LANES), lambda i, t, j: (0, 0))],
        out_specs=[pl.BlockSpec((tb, tg, tn), lambda i, t, j: (i, t, j)),
                   pl.BlockSpec((tb, tg, LANES), lambda i, t, j: (i, t, 0))],
        out_shape=[jax.ShapeDtypeStruct((nb, g, width), F32),
                   jax.ShapeDtypeStruct((nb, g, LANES), F32)],
        scratch_shapes=[pltpu.VMEM((tb * tg, d), BF16)],
        compiler_params=_params(("parallel", "parallel", "arbitrary")),
        name="inproj",
    )(x3, ada3, ada3, w_main, w_f, f_bias)


def _aug_constants(heads, dh):
    width = heads * LANES
    pq = np.zeros((3, LANES, width), np.float32)
    pk = np.zeros((3, LANES, width), np.float32)
    ones_q = np.zeros((1, width), np.float32)
    ones_k = np.zeros((1, width), np.float32)
    own = np.zeros((1, width), np.float32)
    for h in range(heads):
        o_h = (h % 2) * dh
        a_h = h * LANES + (dh - o_h)
        own[0, h * LANES + o_h:h * LANES + o_h + dh] = 1.0
        for i in range(3):
            pq[i, h, a_h + i] = 1.0
            pk[i, h, a_h + 3 + i] = 1.0
            ones_q[0, a_h + 3 + i] = 1.0
            ones_k[0, a_h + i] = 1.0
    return (jnp.asarray(pq, BF16), jnp.asarray(pk, BF16), jnp.asarray(ones_q), jnp.asarray(ones_k),
            jnp.asarray(own))


def _prep_kernel(q_ref, k_ref, v_ref, lf_ref, pq_ref, pk_ref, oq_ref, ok_ref, own_ref,
                 qa_ref, ka_ref, va_ref, carry, *, scale):
    tc = q_ref.shape[1]
    heads = qa_ref.shape[2] // LANES

    @pl.when(pl.program_id(1) == 0)
    def _():
        carry[...] = jnp.zeros_like(carry)

    d = _dot3(_tri_lower(tc), lf_ref[0]) + carry[...]
    carry[...] = d[tc - 1:tc, :]
    hi, mid, lo = _split3(d * LOG2_E)
    aux_q = _dot(hi, pq_ref[0]) + _dot(mid, pq_ref[1]) + _dot(lo, pq_ref[2]) + oq_ref[...]
    aux_k = ok_ref[...] - (_dot(hi, pk_ref[0]) + _dot(mid, pk_ref[1]) + _dot(lo, pk_ref[2]))
    own = own_ref[...] > 0.5

    def widen(x):
        return jnp.concatenate([x[:, (h // 2) * LANES:(h // 2 + 1) * LANES] for h in range(heads)], axis=1)

    qa_ref[0] = jnp.where(own, widen(q_ref[0]) * (scale * LOG2_E), aux_q).astype(BF16)
    ka_ref[0] = jnp.where(own, widen(k_ref[0]), aux_k).astype(BF16)
    dh = LANES // 2
    row = lax.broadcasted_iota(jnp.int32, (LANES, tc), 0)
    v = v_ref[0]
    for hp in range(heads // 2):
        vt = v[:, hp * LANES:(hp + 1) * LANES].T
        va_ref[0, (2 * hp) * LANES:(2 * hp + 1) * LANES, :] = jnp.where(
            row < dh, vt, jnp.where(row == dh, 1.0, 0.0)).astype(BF16)
        va_ref[0, (2 * hp + 1) * LANES:(2 * hp + 2) * LANES, :] = jnp.where(
            row >= dh, vt, jnp.where(row == 0, 1.0, 0.0)).astype(BF16)


def _attn_prep(proj, logf, consts, cols, heads, dh):
    b, t, _ = proj.shape
    hw = heads * dh
    width = heads * LANES
    tc = _largest_divisor(t, 512, SUBLANES)
    pq, pk, oq, ok, own = consts
    cq, ck, cv = cols
    full = lambda shape: pl.BlockSpec(shape, lambda i, j: (0,) * len(shape))
    out = jax.ShapeDtypeStruct((b, t, width), BF16)
    return pl.pallas_call(
        functools.partial(_prep_kernel, scale=dh ** -0.5),
        grid=(b, t // tc),
        in_specs=[pl.BlockSpec((1, tc, hw), lambda i, j: (i, j, cq)),
                  pl.BlockSpec((1, tc, hw), lambda i, j: (i, j, ck)),
                  pl.BlockSpec((1, tc, hw), lambda i, j: (i, j, cv)),
                  pl.BlockSpec((1, tc, LANES), lambda i, j: (i, j, 0)),
                  full(pq.shape), full(pk.shape), full(oq.shape), full(ok.shape), full(own.shape)],
        out_specs=[pl.BlockSpec((1, tc, width), lambda i, j: (i, j, 0)),
                   pl.BlockSpec((1, tc, width), lambda i, j: (i, j, 0)),
                   pl.BlockSpec((1, width, tc), lambda i, j: (i, 0, j))],
        out_shape=[out, out, jax.ShapeDtypeStruct((b, width, t), BF16)],
        scratch_shapes=[pltpu.VMEM((1, LANES), F32)],
        compiler_params=_params(("parallel", "arbitrary")),
        name="attn_prep",
    )(proj, proj, proj, logf, pq, pk, oq, ok, own)


def _attn_kernel(qi_ref, kj_ref, q_ref, k_ref, v_ref, o_ref, m_scr, acc_scr, *, dh):
    tq = q_ref.shape[1]
    tk = k_ref.shape[1]
    heads = q_ref.shape[2] // LANES
    step = pl.program_id(1)
    i = qi_ref[step]
    j = kj_ref[step]

    @pl.when(j == 0)
    def _():
        m_scr[...] = jnp.full_like(m_scr, MASK_VALUE)
        acc_scr[...] = jnp.zeros_like(acc_scr)

    def update(masked):
        if masked:
            key = lax.broadcasted_iota(jnp.int32, (tk, tq), 0)
            qry = lax.broadcasted_iota(jnp.int32, (tk, tq), 1)
            keep = key <= qry

        def scores(h):
            blk = slice(h * LANES, (h + 1) * LANES)
            s = lax.dot_general(k_ref[0, :, blk], q_ref[0, :, blk], NT_DIMS, preferred_element_type=F32)
            if masked:
                s = jnp.where(keep, s, MASK_VALUE)
            m_prev = m_scr[h]
            return s, m_prev, jnp.maximum(m_prev, jnp.max(s, axis=0, keepdims=True))

        def accumulate(h, s, m_prev, m_new):
            p = jnp.exp2(s - m_new).astype(BF16)
            acc_scr[h] = jnp.exp2(m_prev - m_new) * acc_scr[h] + _dot(v_ref[0, h * LANES:(h + 1) * LANES, :], p)
            m_scr[h] = m_new

        pending = scores(0)
        for h in range(heads):
            ahead = scores(h + 1) if h + 1 < heads else None
            accumulate(h, *pending)
            pending = ahead

    @pl.when(j < i)
    def _():
        update(False)

    @pl.when(j == i)
    def _():
        update(True)
        row = lax.broadcasted_iota(jnp.int32, (LANES, tq), 0)
        for hp in range(heads // 2):
            a0 = acc_scr[2 * hp]
            a1 = acc_scr[2 * hp + 1]
            o_t = jnp.where(row < dh, a0 / a0[dh:dh + 1, :], a1 / a1[0:1, :])
            o_ref[0, :, hp * LANES:(hp + 1) * LANES] = o_t.T.astype(o_ref.dtype)


def _attn_prompt(qa, ka, va, heads, dh):
    b, t, width = qa.shape
    tq = _largest_divisor(t, 512, SUBLANES)
    nq = t // tq
    qi = np.array([i for i in range(nq) for j in range(i + 1)], np.int32)
    kj = np.array([j for i in range(nq) for j in range(i + 1)], np.int32)
    grid_spec = pltpu.PrefetchScalarGridSpec(
        num_scalar_prefetch=2,
        grid=(b, len(qi)),
        in_specs=[pl.BlockSpec((1, tq, width), lambda bb, s, qi_r, kj_r: (bb, qi_r[s], 0)),
                  pl.BlockSpec((1, tq, width), lambda bb, s, qi_r, kj_r: (bb, kj_r[s], 0)),
                  pl.BlockSpec((1, width, tq), lambda bb, s, qi_r, kj_r: (bb, 0, kj_r[s]))],
        out_specs=pl.BlockSpec((1, tq, heads * dh), lambda bb, s, qi_r, kj_r: (bb, qi_r[s], 0)),
        scratch_shapes=[pltpu.VMEM((heads, 1, tq), F32), pltpu.VMEM((heads, LANES, tq), F32)],
    )
    return pl.pallas_call(
        functools.partial(_attn_kernel, dh=dh),
        grid_spec=grid_spec,
        out_shape=jax.ShapeDtypeStruct((b, t, heads * dh), BF16),
        compiler_params=_params(("parallel", "arbitrary")),
        name="attn_prompt",
    )(jnp.asarray(qi), jnp.asarray(kj), qa, ka, va)


def _attn_sample_kernel(pt_ref, q_ref, kc_ref, vc_ref, lfc_ref, ck_hbm, cv_hbm, clf_hbm, o_ref,
                        kbuf, vbuf, lbuf, sems, *, layer, n_pages, heads, dh):
    b = pl.program_id(0)
    nb = pl.num_programs(0)
    page = lbuf.shape[2]
    hw = heads * dh
    t_cur = q_ref.shape[1]
    rows = heads * t_cur
    npr = n_pages * heads

    def copies(seq, slot):
        out = []
        for p in range(n_pages):
            pid = pt_ref[seq, p]
            lanes = pl.ds(p * page, page)
            out.append(pltpu.make_async_copy(ck_hbm.at[layer, pid], kbuf.at[slot, :, lanes], sems.at[0, slot, p]))
            out.append(pltpu.make_async_copy(cv_hbm.at[layer, pid], vbuf.at[slot, :, lanes], sems.at[1, slot, p]))
            out.append(pltpu.make_async_copy(clf_hbm.at[layer, pid], lbuf.at[slot, pl.ds(p * heads, heads), :],
                                             sems.at[2, slot, p]))
        return out

    @pl.when(b == 0)
    def _():
        for c in copies(0, 0):
            c.start()

    slot = b % 2

    @pl.when(b + 1 < nb)
    def _():
        for c in copies(b + 1, 1 - slot):
            c.start()

    for c in copies(b, slot):
        c.wait()

    ur = lax.broadcasted_iota(jnp.int32, (page, page), 0)
    uc = lax.broadcasted_iota(jnp.int32, (page, page), 1)
    triu = jnp.where(ur <= uc, 1.0, 0.0).astype(BF16)
    l_hi, l_mid, l_lo = _split3(lbuf[slot])
    cum = _dot(l_hi, triu) + _dot(l_mid, triu) + _dot(l_lo, triu)
    mr = lax.broadcasted_iota(jnp.int32, (npr, npr), 0)
    mc = lax.broadcasted_iota(jnp.int32, (npr, npr), 1)
    later = jnp.where((mr % heads == mc % heads) & (mc // heads >= mr // heads), 1.0, 0.0).astype(BF16)
    e = cum - _dot3(later, jnp.broadcast_to(cum[:, page - 1:page], (npr, page)))
    rr = lax.broadcasted_iota(jnp.int32, (n_pages * rows, npr), 0)
    rc = lax.broadcasted_iota(jnp.int32, (n_pages * rows, npr), 1)
    spread = jnp.where((rr // rows == rc // heads) & ((rr % rows) // t_cur == rc % heads), -1.0, 0.0).astype(BF16)
    bias_rows = _dot3(spread, e)
    bias = jnp.concatenate([bias_rows[p * rows:(p + 1) * rows] for p in range(n_pages)], axis=1)

    cc = _dot3(_tri_lower(t_cur), lfc_ref[0])
    rrow = lax.broadcasted_iota(jnp.int32, (rows, hw), 0) // t_cur
    rcol = lax.broadcasted_iota(jnp.int32, (rows, hw), 1) // dh
    q_rep = jnp.concatenate([q_ref[0]] * heads, axis=0)
    qbd = jnp.where(rrow == rcol, q_rep * (dh ** -0.5), 0.0).astype(BF16)
    arow = lax.broadcasted_iota(jnp.int32, (rows, LANES), 0) // t_cur
    acol = lax.broadcasted_iota(jnp.int32, (rows, LANES), 1)
    sel = arow == acol
    neg_sel = jnp.where(sel, -1.0, 0.0).astype(BF16)
    cc_rep = jnp.concatenate([cc] * heads, axis=0)
    dq = jnp.sum(jnp.where(sel, cc_rep, 0.0), axis=-1, keepdims=True)

    nt = lambda a, bm: lax.dot_general(a, bm, NT_DIMS, preferred_element_type=F32)
    s_past = _dot(qbd, kbuf[slot].astype(BF16)) + bias + dq
    c_hi, c_mid, c_lo = _split3(cc)
    s_cur = nt(qbd, kc_ref[0].astype(BF16)) + (nt(neg_sel, c_hi) + nt(neg_sel, c_mid) + nt(neg_sel, c_lo)) + dq
    tq_idx = lax.broadcasted_iota(jnp.int32, (rows, t_cur), 0) % t_cur
    ts_idx = lax.broadcasted_iota(jnp.int32, (rows, t_cur), 1)
    s_cur = jnp.where(ts_idx <= tq_idx, s_cur, MASK_VALUE)

    m = jnp.maximum(jnp.max(s_past, axis=-1, keepdims=True), jnp.max(s_cur, axis=-1, keepdims=True))
    p_past = jnp.exp(s_past - m)
    p_cur = jnp.exp(s_cur - m)
    denom = jnp.sum(p_past, axis=-1, keepdims=True) + jnp.sum(p_cur, axis=-1, keepdims=True)
    o_all = (nt(p_past.astype(BF16), vbuf[slot].astype(BF16))
             + _dot(p_cur.astype(BF16), vc_ref[0].astype(BF16))) / denom
    o_all = jnp.where(rrow == rcol, o_all, 0.0)
    o = o_all[0:t_cur]
    for h in range(1, heads):
        o = o + o_all[h * t_cur:(h + 1) * t_cur]
    o_ref[0] = o.astype(o_ref.dtype)


def _attn_sample(proj, logf, cache_k, cache_v, cache_logf, page_table, layer, cols, heads, dh):
    nb, t_cur, _ = proj.shape
    hw = heads * dh
    depth, n_pool, page = cache_k.shape[:3]
    n_pages = page_table.shape[1]
    ck = jnp.transpose(cache_k, (0, 1, 3, 4, 2)).reshape(depth, n_pool, hw, page)
    cv = jnp.transpose(cache_v, (0, 1, 3, 4, 2)).reshape(depth, n_pool, hw, page)
    clf = jnp.transpose(cache_logf, (0, 1, 3, 2))
    cq, ckc, cvc = cols
    grid_spec = pltpu.PrefetchScalarGridSpec(
        num_scalar_prefetch=1,
        grid=(nb,),
        in_specs=[pl.BlockSpec((1, t_cur, hw), lambda b, pt: (b, 0, cq)),
                  pl.BlockSpec((1, t_cur, hw), lambda b, pt: (b, 0, ckc)),
                  pl.BlockSpec((1, t_cur, hw), lambda b, pt: (b, 0, cvc)),
                  pl.BlockSpec((1, t_cur, LANES), lambda b, pt: (b, 0, 0)),
                  pl.BlockSpec(memory_space=pl.ANY),
                  pl.BlockSpec(memory_space=pl.ANY),
                  pl.BlockSpec(memory_space=pl.ANY)],
        out_specs=pl.BlockSpec((1, t_cur, hw), lambda b, pt: (b, 0, 0)),
        scratch_shapes=[pltpu.VMEM((2, hw, n_pages * page), F32),
                        pltpu.VMEM((2, hw, n_pages * page), F32),
                        pltpu.VMEM((2, n_pages * heads, page), F32),
                        pltpu.SemaphoreType.DMA((3, 2, n_pages))],
    )
    return pl.pallas_call(
        functools.partial(_attn_sample_kernel, layer=layer, n_pages=n_pages, heads=heads, dh=dh),
        grid_spec=grid_spec,
        out_shape=jax.ShapeDtypeStruct((nb, t_cur, hw), BF16),
        compiler_params=_params(("arbitrary",)),
        name="attn_sample",
    )(page_table, proj, proj, proj, logf, ck, cv, clf)


def _hgrn_kernel(q_ref, f_ref, v_ref, og_ref, lb_ref, ng_ref, s0_ref, y_ref, sout_ref, st_scr, *, chunk, sub):
    tb, tt, width = q_ref.shape
    heads = s0_ref.shape[1]
    kd = width // heads
    n_chunks = tt // chunk
    n_sub = chunk // sub
    t_idx = pl.program_id(1)
    hb = [slice(h * kd, (h + 1) * kd) for h in range(heads)]
    per_head = lambda fn: jnp.concatenate([fn(h) for h in range(heads)], axis=1)

    @pl.when(t_idx == 0)
    def _():
        for ib in range(tb):
            for h in range(heads):
                st_scr[ib, h] = s0_ref[ib, h].T

    lb = lb_ref[...]
    ng = ng_ref[...]
    tri = _tri_lower(chunk)
    sub_col = lax.broadcasted_iota(jnp.int32, (sub, 1), 0)

    def do_chunk(ib, rows):
        g = lb + (1.0 - lb) * _sigmoid(f_ref[ib, rows, :])
        kk = 1.0 - g
        qq = _silu(q_ref[ib, rows, :])
        v = v_ref[ib, rows, :]
        vb = v.astype(BF16)
        bcum = _dot3(tri, jnp.log(g))
        st = [st_scr[ib, h] for h in range(heads)]
        qe = (qq * jnp.exp(bcum)).astype(BF16)
        nt = lambda x, y: lax.dot_general(x, y, NT_DIMS, preferred_element_type=F32)
        o_inter = [nt(qe[:, hb[h]], st[h].astype(BF16)) for h in range(heads)]
        b_last = bcum[chunk - 1:chunk]
        kdec = (kk * jnp.exp(b_last - bcum)).astype(BF16)
        e_last = jnp.exp(b_last)
        st_add = [lax.dot_general(vb[:, hb[h]], kdec[:, hb[h]], TN_DIMS, preferred_element_type=F32)
                  for h in range(heads)]
        cross_sc = {}
        for si in range(1, n_sub):
            lo_r, hi_r = si * sub, (si + 1) * sub
            ref = bcum[lo_r - 1:lo_r]
            a = (qq[lo_r:hi_r] * jnp.exp(bcum[lo_r:hi_r] - ref)).astype(BF16)
            bm = (kk[:lo_r] * jnp.exp(ref - bcum[:lo_r])).astype(BF16)
            cross_sc[si] = [nt(a[:, hb[h]], bm[:, hb[h]]) for h in range(heads)]
        diag = []
        for si in range(n_sub):
            lo_r, hi_r = si * sub, (si + 1) * sub
            qs, ks, vs, bs2 = qq[lo_r:hi_r], kk[lo_r:hi_r], v[lo_r:hi_r], bcum[lo_r:hi_r] * LOG2_E
            acc = None
            for s in range(sub):
                y = qs * ks[s:s + 1] * jnp.exp2(bs2 - bs2[s:s + 1])
                term = per_head(lambda h: jnp.where(sub_col >= s, jnp.sum(y[:, hb[h]], axis=-1, keepdims=True),
                                                    0.0) * vs[s:s + 1, hb[h]])
                acc = term if acc is None else acc + term
            diag.append(acc)
        pieces = []
        for si in range(n_sub):
            lo_r, hi_r = si * sub, (si + 1) * sub
            acc = diag[si] + per_head(lambda h: o_inter[h][lo_r:hi_r])
            if si > 0:
                acc = acc + per_head(lambda h: _dot(cross_sc[si][h].astype(BF16), vb[:lo_r, hb[h]]))
            pieces.append(acc)
        o = jnp.concatenate(pieces, axis=0) if n_sub > 1 else pieces[0]
        for h in range(heads):
            st_scr[ib, h] = e_last[:, hb[h]] * st[h] + st_add[h]
        o = per_head(lambda h: o[:, hb[h]] * lax.rsqrt(jnp.mean(o[:, hb[h]] * o[:, hb[h]], axis=-1, keepdims=True)
                                                       + RMS_EPS))
        y_ref[ib, rows, :] = (o * ng * _silu(og_ref[ib, rows, :])).astype(y_ref.dtype)

    for ib in range(tb):
        if n_chunks == 1:
            do_chunk(ib, slice(0, chunk))
        else:
            def body(c, carry, ib=ib):
                do_chunk(ib, pl.ds(pl.multiple_of(c * chunk, chunk), chunk))
                return carry
            lax.fori_loop(0, n_chunks, body, 0)

    @pl.when(t_idx == pl.num_programs(1) - 1)
    def _():
        for ib in range(tb):
            for h in range(heads):
                sout_ref[ib, h] = st_scr[ib, h].T


def _hgrn(proj, s0, lb, norm_g, cols, hg_heads):
    nb, g, _ = proj.shape
    kd = s0.shape[-2]
    vd = s0.shape[-1]
    width = hg_heads * kd
    if g >= 64:
        tb, tt = 1, _largest_divisor(g, 512, 64)
        chunk, sub = 64, 8
    else:
        tb, tt = _largest_divisor(nb, 8, 1), g
        chunk, sub = g, g
    col = lambda c0: pl.BlockSpec((tb, tt, width), lambda i, t: (i, t, c0))
    return pl.pallas_call(
        functools.partial(_hgrn_kernel, chunk=chunk, sub=sub),
        grid=(nb // tb, g // tt),
        in_specs=[col(cols[0]), col(cols[1]), col(cols[2]), col(cols[3]),
                  pl.BlockSpec((1, width), lambda i, t: (0, 0)),
                  pl.BlockSpec((1, width), lambda i, t: (0, 0)),
                  pl.BlockSpec((tb, hg_heads, kd, vd), lambda i, t: (i, 0, 0, 0))],
        out_specs=[pl.BlockSpec((tb, tt, width), lambda i, t: (i, t, 0)),
                   pl.BlockSpec((tb, hg_heads, kd, vd), lambda i, t: (i, 0, 0, 0))],
        out_shape=[jax.ShapeDtypeStruct((nb, g, width), BF16),
                   jax.ShapeDtypeStruct(s0.shape, F32)],
        scratch_shapes=[pltpu.VMEM((tb, hg_heads, vd, kd), F32)],
        compiler_params=_params(("parallel", "arbitrary")),
        name="hgrn",
    )(proj, proj, proj, proj, lb.reshape(1, -1), norm_g.reshape(1, -1), s0)


def _merge_kernel(ya_ref, yb_ref, ga_ref, gb_ref, x_ref, gm_ref, scf_ref, shf_ref,
                  wa_ref, wb_ref, wo_ref, lg_ref, lbias_ref, x1_ref, h2_ref, *, alpha):
    tb, tg, d = x_ref.shape
    tm = tb * tg
    y_a = _dot(ya_ref[...].reshape(tm, -1), wa_ref[...])
    y_b = _dot(yb_ref[...].reshape(tm, -1), wb_ref[...])
    merged = _sigmoid(ga_ref[...].reshape(tm, d)) * y_a + _sigmoid(gb_ref[...].reshape(tm, d)) * y_b
    z = _dot(merged.astype(BF16), wo_ref[...]).reshape(tb, tg, d)
    x1 = _layer_norm(alpha * x_ref[...] + (1.0 + gm_ref[...]) * z, lg_ref[...], lbias_ref[...])
    x1_ref[...] = x1
    h2_ref[...] = (x1 * (1.0 + scf_ref[...]) + shf_ref[...]).astype(h2_ref.dtype)


def _merge(attn, yb, proj, x3, ada3, w_br_a, w_br_b, w_o, ln_g, ln_b, alpha, gate_cols, rows=512):
    nb, g, d = x3.shape
    tb, tg = _row_tiles(nb, g, rows)
    ca, cb = gate_cols
    wa = attn.shape[-1]
    wb = yb.shape[-1]
    tok = lambda w, c: pl.BlockSpec((tb, tg, w), lambda i, t: (i, t, c))
    mod = lambda c: pl.BlockSpec((tb, 1, d), lambda i, t: (i, 0, c))
    full = lambda shape: pl.BlockSpec(shape, lambda i, t: (0,) * len(shape))
    return pl.pallas_call(
        functools.partial(_merge_kernel, alpha=alpha),
        grid=(nb // tb, g // tg),
        in_specs=[tok(wa, 0), tok(wb, 0), tok(d, ca), tok(d, cb), tok(d, 0),
                  mod(2), mod(4), mod(3),
                  full(w_br_a.shape), full(w_br_b.shape), full(w_o.shape), full((1, d)), full((1, d))],
        out_specs=[tok(d, 0), tok(d, 0)],
        out_shape=[jax.ShapeDtypeStruct((nb, g, d), F32), jax.ShapeDtypeStruct((nb, g, d), BF16)],
        compiler_params=_params(("parallel", "parallel")),
        name="merge",
    )(attn, yb, proj, proj, x3, ada3, ada3, ada3, w_br_a, w_br_b, w_o, ln_g.reshape(1, d), ln_b.reshape(1, d))


def _gelu(x):
    return 0.5 * x * (1.0 + lax.erf(x * (2.0 ** -0.5)))


def _ffn_kernel(h_ref, wu_ref, cw_ref, cb_ref, wd_ref, prev_ref, x1_ref, gf_ref, lg_ref, lbias_ref,
                x2_ref, cnew_ref, act_scr, carry_scr, *, alpha, tf):
    tb, tg, d = h_ref.shape
    tm = tb * tg
    dff = wd_ref.shape[0]
    first = pl.program_id(1) == 0
    h = h_ref[...].reshape(tm, d)

    @pl.when(first)
    def _():
        carry_scr[...] = jnp.zeros_like(carry_scr)

    tt = lax.broadcasted_iota(jnp.int32, (tb, tg, tf), 1)
    for c in range(dff // tf):
        cols = slice(c * tf, (c + 1) * tf)
        a = _dot(h, wu_ref[:, cols])
        gate = _dot(h, wu_ref[:, dff + c * tf:dff + (c + 1) * tf])
        a3 = a.reshape(tb, tg, tf)
        tail = carry_scr[:, :, cols]
        p0 = jnp.where(first, prev_ref[:, 0:1, cols], tail[:, SUBLANES - 2:SUBLANES - 1, :])
        p1 = jnp.where(first, prev_ref[:, 1:2, cols], tail[:, SUBLANES - 1:SUBLANES, :])
        am1 = jnp.where(tt == 0, p1, pltpu.roll(a, 1, 0).reshape(tb, tg, tf))
        am2 = jnp.where(tt == 0, p0, jnp.where(tt == 1, p1, pltpu.roll(a, 2, 0).reshape(tb, tg, tf)))
        conv = cb_ref[:, cols] + cw_ref[0:1, cols] * am2 + cw_ref[1:2, cols] * am1 + cw_ref[2:3, cols] * a3
        act_scr[:, cols] = (_gelu(conv) * gate.reshape(tb, tg, tf)).astype(BF16).reshape(tm, tf)
        carry_scr[:, :, cols] = a3[:, tg - SUBLANES:, :]
        cnew_ref[:, 0, :, cols] = a3[:, tg - 2:, :]

    y = _dot(act_scr[...], wd_ref[...]).reshape(tb, tg, d)
    x2_ref[...] = _layer_norm(alpha * x1_ref[...] + (1.0 + gf_ref[...]) * y, lg_ref[...], lbias_ref[...])


def _ffn(h2, x1, ada3, prev, w_up, conv_w, conv_b, w_down, ln_g, ln_b, alpha, rows=512, tf=256):
    nb, g, d = x1.shape
    dff = w_down.shape[0]
    tb, tg = _row_tiles(nb, g, rows)
    resident = lambda shape: pl.BlockSpec(shape, lambda i, t: (0,) * len(shape), pipeline_mode=pl.Buffered(1))
    return pl.pallas_call(
        functools.partial(_ffn_kernel, alpha=alpha, tf=tf),
        grid=(nb // tb, g // tg),
        in_specs=[pl.BlockSpec((tb, tg, d), lambda i, t: (i, t, 0)),
                  resident(w_up.shape),
                  resident(conv_w.shape),
                  resident((1, dff)),
                  resident(w_down.shape),
                  pl.BlockSpec((tb, 2, dff), lambda i, t: (i, 0, 0)),
                  pl.BlockSpec((tb, tg, d), lambda i, t: (i, t, 0)),
                  pl.BlockSpec((tb, 1, d), lambda i, t: (i, 0, 5)),
                  resident((1, d)),
                  resident((1, d))],
        out_specs=[pl.BlockSpec((tb, tg, d), lambda i, t: (i, t, 0)),
                   pl.BlockSpec((tb, 1, 2, dff), lambda i, t: (i, t, 0, 0))],
        out_shape=[jax.ShapeDtypeStruct((nb, g, d), F32), jax.ShapeDtypeStruct((nb, g // tg, 2, dff), F32)],
        scratch_shapes=[pltpu.VMEM((tb * tg, dff), BF16), pltpu.VMEM((tb, SUBLANES, dff), F32)],
        compiler_params=_params(("parallel", "arbitrary")),
        name="ffn",
    )(h2, w_up, conv_w, conv_b.reshape(1, dff), w_down, prev, x1, ada3,
      ln_g.reshape(1, d), ln_b.reshape(1, d))


def kernel(x_prompt, x_sample, cache_k, cache_v, cache_logf, state_hgrn, state_conv, page_table, c_prompt, c_sample, ln0_g, ln0_b, w_in, fox_f_bias, hg_lb, hg_norm_g, w_br_a, w_br_b, w_o, ln1_g, ln1_b, w_up, conv_w, conv_b, w_down, ln2_g, ln2_b, w_ada, b_ada):
    bp, tp, d = x_prompt.shape
    bs, ts, _ = x_sample.shape
    depth = w_in.shape[0]
    heads, dh = cache_k.shape[3], cache_k.shape[4]
    hw = heads * dh
    hg_heads, kd, vd = state_hgrn.shape[2:]
    hgw = hg_heads * kd
    dff = w_down.shape[1]
    assert conv_w.shape[1] == 3 and state_conv.shape[2] == 2
    assert kd == LANES and vd == LANES and 2 * dh == LANES and hw == hgw == hg_heads * vd and d == 2 * hw
    alpha = (2 * depth) ** 0.25

    lb_soft = jax.nn.softmax(hg_lb.astype(F32), axis=0)
    lb_all = jnp.cumsum(lb_soft, axis=0) - lb_soft[0]

    o_f = 3 * hw
    o_b = o_f + heads
    o_g = o_b + 4 * hgw
    w_main = jnp.concatenate([w_in[:, :, o_g:], w_in[:, :, :o_f], w_in[:, :, o_b:o_g]], axis=2).astype(BF16)
    w_f = jnp.pad(w_in[:, :, o_f:o_b], ((0, 0), (0, 0), (0, LANES - heads))).astype(BF16)
    f_bias = jnp.pad(fox_f_bias, ((0, 0), (0, LANES - heads))).reshape(depth, 1, LANES)
    gate_cols = (0, 1)
    qkv_cols = (4, 5, 6)
    hg_cols = tuple((2 * d + 3 * hw) // hgw + i for i in range(4))
    k_off, v_off = 2 * d + hw, 2 * d + 2 * hw
    w_br_a16, w_br_b16, w_o16 = w_br_a.astype(BF16), w_br_b.astype(BF16), w_o.astype(BF16)
    w_up16, w_down16 = w_up.astype(BF16), w_down.astype(BF16)
    consts = _aug_constants(heads, dh)

    ada = _ada(jnp.concatenate([c_prompt, c_sample], axis=0), w_ada, b_ada)
    x_all = _ln0(jnp.concatenate([x_prompt.reshape(bp * tp, d), x_sample.reshape(bs * ts, d)], axis=0), ln0_g, ln0_b)
    xp = x_all[:bp * tp].reshape(bp, tp, d)
    xs = x_all[bp * tp:].reshape(bs, ts, d)
    s0_p = jnp.zeros((bp,) + state_hgrn.shape[2:], F32)
    conv0_p = jnp.zeros((bp, 2, dff), F32)

    outs_p, outs_s = [], []
    for l in range(depth):
        ada_p = ada[l, :bp].reshape(bp, 1, 6 * d)
        ada_s = ada[l, bp:].reshape(bs, 1, 6 * d)
        new = []
        for grp, (x3, ada3) in enumerate(((xp, ada_p), (xs, ada_s))):
            proj, logf = _inproj(x3, ada3, w_main[l], w_f[l], f_bias[l])
            if grp == 0:
                attn = _attn_prompt(*_attn_prep(proj, logf, consts, qkv_cols, heads, dh), heads, dh)
                s0, prev = s0_p, conv0_p
            else:
                attn = _attn_sample(proj, logf, cache_k, cache_v, cache_logf, page_table, l, qkv_cols, heads, dh)
                s0, prev = state_hgrn[l], state_conv[l]
            yb, s_new = _hgrn(proj, s0, lb_all[l], hg_norm_g[l], hg_cols, hg_heads)
            x1, h2 = _merge(attn, yb, proj, x3, ada3, w_br_a16[l], w_br_b16[l], w_o16[l], ln1_g[l], ln1_b[l],
                            alpha, gate_cols)
            x2, conv_new = _ffn(h2, x1, ada3, prev, w_up16[l], conv_w[l], conv_b[l], w_down16[l],
                                ln2_g[l], ln2_b[l], alpha)
            nbx, gx = x3.shape[:2]
            new.append((x2,
                        proj[:, :, k_off:k_off + hw].reshape(nbx, gx, heads, dh),
                        proj[:, :, v_off:v_off + hw].reshape(nbx, gx, heads, dh),
                        logf[:, :, :heads], s_new, conv_new[:, -1]))
        xp, xs = new[0][0], new[1][0]
        outs_p.append(new[0][1:])
        outs_s.append(new[1][1:])

    stack = lambda outs, i: jnp.stack([o[i] for o in outs])
    return (xp, xs,
            stack(outs_p, 0), stack(outs_p, 1), stack(outs_p, 2), stack(outs_p, 3), stack(outs_p, 4),
            stack(outs_s, 0), stack(outs_s, 1), stack(outs_s, 2), stack(outs_s, 3), stack(outs_s, 4))
```

```python
import functools

import numpy as np
import jax
import jax.numpy as jnp
from jax import lax
from jax.experimental import pallas as pl
from jax.experimental.pallas import tpu as pltpu

F32 = jnp.float32
BF16 = jnp.bfloat16

LN_EPS = 1e-5
RMS_EPS = 1e-6
MASK_VALUE = -1e30
LOG2_E = 1.4426950408889634
LANES = 128
SUBLANES = 8
VMEM_LIMIT = 56 * 1024 * 1024

NT_DIMS = (((1,), (1,)), ((), ()))
TN_DIMS = (((0,), (0,)), ((), ()))


def _params(sem):
    return pltpu.CompilerParams(dimension_semantics=sem, vmem_limit_bytes=VMEM_LIMIT)


def _largest_divisor(n, target, mult):
    best = None
    for d in range(1, n + 1):
        if n % d == 0 and d <= target and d % mult == 0:
            best = d
    return n if best is None else best


def _row_tiles(nb, g, target):
    if g >= target:
        return 1, _largest_divisor(g, target, SUBLANES)
    return _largest_divisor(nb, max(1, target // g), 1), g


def _split3(x):
    hi = x.astype(BF16)
    r1 = x - hi.astype(F32)
    mid = r1.astype(BF16)
    lo = (r1 - mid.astype(F32)).astype(BF16)
    return hi, mid, lo


def _dot(a, b):
    return jnp.dot(a, b, preferred_element_type=F32)


def _dot3(a_bf16, x_f32):
    hi, mid, lo = _split3(x_f32)
    return _dot(a_bf16, hi) + _dot(a_bf16, mid) + _dot(a_bf16, lo)


def _layer_norm(x, g, b):
    mu = jnp.mean(x, axis=-1, keepdims=True)
    xc = x - mu
    var = jnp.mean(xc * xc, axis=-1, keepdims=True)
    return xc * lax.rsqrt(var + LN_EPS) * g + b


def _sigmoid(x):
    return 1.0 / (1.0 + jnp.exp(-x))


def _silu(x):
    return x * _sigmoid(x)


def _log_sigmoid(x):
    return jnp.minimum(x, 0.0) - jnp.log1p(jnp.exp(-jnp.abs(x)))


def _tri_lower(n):
    r = lax.broadcasted_iota(jnp.int32, (n, n), 0)
    c = lax.broadcasted_iota(jnp.int32, (n, n), 1)
    return jnp.where(r >= c, 1.0, 0.0).astype(BF16)


def _ln0_kernel(x_ref, g_ref, b_ref, o_ref):
    o_ref[...] = _layer_norm(x_ref[...], g_ref[...], b_ref[...])


def _ln0(x2, g, b):
    n, d = x2.shape
    tm = _largest_divisor(n, 1024, SUBLANES)
    return pl.pallas_call(
        _ln0_kernel,
        grid=(n // tm,),
        in_specs=[pl.BlockSpec((tm, d), lambda i: (i, 0)),
                  pl.BlockSpec((1, d), lambda i: (0, 0)),
                  pl.BlockSpec((1, d), lambda i: (0, 0))],
        out_specs=pl.BlockSpec((tm, d), lambda i: (i, 0)),
        out_shape=jax.ShapeDtypeStruct((n, d), F32),
        compiler_params=_params(("parallel",)),
        name="ln0",
    )(x2, g.reshape(1, d), b.reshape(1, d))


def _ada_kernel(c_ref, w_ref, b_ref, o_ref):
    s = _silu(c_ref[...]).astype(BF16)
    o_ref[0] = _dot(s, w_ref[0].astype(BF16)) + b_ref[0]


def _ada(c_all, w_ada, b_ada):
    depth, d, n6 = w_ada.shape
    nc = c_all.shape[0]
    tn = _largest_divisor(n6, 1536, LANES)
    return pl.pallas_call(
        _ada_kernel,
        grid=(depth, n6 // tn),
        in_specs=[pl.BlockSpec((nc, d), lambda l, j: (0, 0)),
                  pl.BlockSpec((1, d, tn), lambda l, j: (l, 0, j)),
                  pl.BlockSpec((1, 1, tn), lambda l, j: (l, 0, j))],
        out_specs=pl.BlockSpec((1, nc, tn), lambda l, j: (l, 0, j)),
        out_shape=jax.ShapeDtypeStruct((depth, nc, n6), F32),
        compiler_params=_params(("parallel", "parallel")),
        name="ada",
    )(c_all, w_ada, b_ada.reshape(depth, 1, n6))


def _inproj_kernel(x_ref, sh_ref, sc_ref, w_ref, wf_ref, fb_ref, proj_ref, logf_ref, h_scr):
    tb, tg, d = x_ref.shape

    @pl.when(pl.program_id(2) == 0)
    def _():
        h = x_ref[...] * (1.0 + sc_ref[...]) + sh_ref[...]
        h_scr[...] = h.reshape(tb * tg, d).astype(BF16)
        f = _dot(h_scr[...], wf_ref[...]) + fb_ref[...]
        logf_ref[...] = _log_sigmoid(f).reshape(tb, tg, LANES)

    proj_ref[...] = _dot(h_scr[...], w_ref[...]).reshape(tb, tg, -1)


def _inproj(x3, ada3, w_main, w_f, f_bias, width, tn, rows=1024):
    nb, g, d = x3.shape
    tb, tg = _row_tiles(nb, g, rows)
    return pl.pallas_call(
        _inproj_kernel,
        grid=(nb // tb, g // tg, width // tn),
        in_specs=[pl.BlockSpec((tb, tg, d), lambda i, t, j: (i, t, 0)),
                  pl.BlockSpec((tb, 1, d), lambda i, t, j: (i, 0, 0)),
                  pl.BlockSpec((tb, 1, d), lambda i, t, j: (i, 0, 1)),
                  pl.BlockSpec((d, tn), lambda i, t, j: (0, j)),
                  pl.BlockSpec((d, LANES), lambda i, t, j: (0, 0)),
                  pl.BlockSpec((1, LANES), lambda i, t, j: (0, 0))],
        out_specs=[pl.BlockSpec((tb, tg, tn), lambda i, t, j: (i, t, j)),
                   pl.BlockSpec((tb, tg, LANES), lambda i, t, j: (i, t, 0))],
        out_shape=[jax.ShapeDtypeStruct((nb, g, width), F32),
                   jax.ShapeDtypeStruct((nb, g, LANES), F32)],
        scratch_shapes=[pltpu.VMEM((tb * tg, d), BF16)],
        compiler_params=_params(("parallel", "parallel", "arbitrary")),
        name="inproj",
    )(x3, ada3, ada3, w_main, w_f, f_bias)


def _aug_constants(heads, dh):
    width = heads * LANES
    pq = np.zeros((3, LANES, width), np.float32)
    pk = np.zeros((3, LANES, width), np.float32)
    ones_q = np.zeros((1, width), np.float32)
    ones_k = np.zeros((1, width), np.float32)
    own = np.zeros((1, width), np.float32)
    for h in range(heads):
        o_h = (h % 2) * dh
        a_h = h * LANES + (dh - o_h)
        own[0, h * LANES + o_h:h * LANES + o_h + dh] = 1.0
        for i in range(3):
            pq[i, h, a_h + i] = 1.0
            pk[i, h, a_h + 3 + i] = 1.0
            ones_q[0, a_h + 3 + i] = 1.0
            ones_k[0, a_h + i] = 1.0
    return (jnp.asarray(pq, BF16), jnp.asarray(pk, BF16), jnp.asarray(ones_q), jnp.asarray(ones_k),
            jnp.asarray(own))


def _prep_kernel(x_ref, sh_ref, sc_ref, wq_ref, wk_ref, wv_ref, lf_ref, pq_ref, pk_ref, oq_ref, ok_ref, own_ref,
                 *rest, scale):
    qa_ref, ka_ref, va_ref, kt_ref, vt_ref, carry = rest[-6:]
    tc = x_ref.shape[1]
    heads = qa_ref.shape[2] // LANES

    @pl.when(pl.program_id(1) == 0)
    def _():
        carry[...] = jnp.zeros_like(carry)

    h = (x_ref[0] * (1.0 + sc_ref[0]) + sh_ref[0]).astype(BF16)
    q = _dot(h, wq_ref[...])
    k = _dot(h, wk_ref[...])
    v = _dot(h, wv_ref[...])

    d = _dot3(_tri_lower(tc), lf_ref[0]) + carry[...]
    carry[...] = d[tc - 1:tc, :]
    hi, mid, lo = _split3(d * LOG2_E)
    aux_q = _dot(hi, pq_ref[0]) + _dot(mid, pq_ref[1]) + _dot(lo, pq_ref[2]) + oq_ref[...]
    aux_k = ok_ref[...] - (_dot(hi, pk_ref[0]) + _dot(mid, pk_ref[1]) + _dot(lo, pk_ref[2]))
    own = own_ref[...] > 0.5

    def widen(x):
        return jnp.concatenate([x[:, (h // 2) * LANES:(h // 2 + 1) * LANES] for h in range(heads)], axis=1)

    qa_ref[0] = jnp.where(own, widen(q) * (scale * LOG2_E), aux_q).astype(BF16)
    ka_ref[0] = jnp.where(own, widen(k), aux_k).astype(BF16)
    dh = LANES // 2
    row = lax.broadcasted_iota(jnp.int32, (LANES, tc), 0)
    for hp in range(heads // 2):
        pair = slice(hp * LANES, (hp + 1) * LANES)
        kt_ref[0, 0, pair, :] = k[:, pair].T
        vt = v[:, pair].T
        vt_ref[0, 0, pair, :] = vt
        va_ref[0, (2 * hp) * LANES:(2 * hp + 1) * LANES, :] = jnp.where(
            row < dh, vt, jnp.where(row == dh, 1.0, 0.0)).astype(BF16)
        va_ref[0, (2 * hp + 1) * LANES:(2 * hp + 2) * LANES, :] = jnp.where(
            row >= dh, vt, jnp.where(row == 0, 1.0, 0.0)).astype(BF16)


def _attn_prep(x3, ada3, w_main, logf, consts, cols, heads, dh, layer, depth, stacked):
    b, t, d = x3.shape
    hw = heads * dh
    width = heads * LANES
    tc = _largest_divisor(t, 512, SUBLANES)
    pq, pk, oq, ok, own = consts
    cq, ck, cv = cols
    full = lambda shape: pl.BlockSpec(shape, lambda i, j: (0,) * len(shape))
    out = jax.ShapeDtypeStruct((b, t, width), BF16)
    cache = jax.ShapeDtypeStruct((depth, b, hw, t), F32)
    cache_spec = pl.BlockSpec((1, 1, hw, tc), lambda i, j: (layer, i, 0, j))
    extra = tuple(stacked)
    n_in = 12
    return pl.pallas_call(
        functools.partial(_prep_kernel, scale=dh ** -0.5),
        grid=(b, t // tc),
        in_specs=[pl.BlockSpec((1, tc, d), lambda i, j: (i, j, 0)),
                  pl.BlockSpec((1, 1, d), lambda i, j: (i, 0, 0)),
                  pl.BlockSpec((1, 1, d), lambda i, j: (i, 0, 1)),
                  pl.BlockSpec((d, hw), lambda i, j: (0, cq)),
                  pl.BlockSpec((d, hw), lambda i, j: (0, ck)),
                  pl.BlockSpec((d, hw), lambda i, j: (0, cv)),
                  pl.BlockSpec((1, tc, LANES), lambda i, j: (i, j, 0)),
                  full(pq.shape), full(pk.shape), full(oq.shape), full(ok.shape), full(own.shape)]
                 + [pl.BlockSpec(memory_space=pl.ANY)] * len(extra),
        out_specs=[pl.BlockSpec((1, tc, width), lambda i, j: (i, j, 0)),
                   pl.BlockSpec((1, tc, width), lambda i, j: (i, j, 0)),
                   pl.BlockSpec((1, width, tc), lambda i, j: (i, 0, j)),
                   cache_spec, cache_spec],
        out_shape=[out, out, jax.ShapeDtypeStruct((b, width, t), BF16), cache, cache],
        input_output_aliases={n_in + i: 3 + i for i in range(len(extra))},
        scratch_shapes=[pltpu.VMEM((1, LANES), F32)],
        compiler_params=_params(("parallel", "arbitrary")),
        name="attn_prep",
    )(x3, ada3, ada3, w_main, w_main, w_main, logf, pq, pk, oq, ok, own, *extra)


def _attn_kernel(qi_ref, kj_ref, q_ref, k_ref, v_ref, o_ref, m_scr, acc_scr, *, dh):
    tq = q_ref.shape[1]
    tk = k_ref.shape[1]
    heads = q_ref.shape[2] // LANES
    step = pl.program_id(1)
    i = qi_ref[step]
    j = kj_ref[step]

    @pl.when(j == 0)
    def _():
        m_scr[...] = jnp.full_like(m_scr, MASK_VALUE)
        acc_scr[...] = jnp.zeros_like(acc_scr)

    def update(masked):
        if masked:
            key = lax.broadcasted_iota(jnp.int32, (tk, tq), 0)
            qry = lax.broadcasted_iota(jnp.int32, (tk, tq), 1)
            keep = key <= qry

        def scores(h):
            blk = slice(h * LANES, (h + 1) * LANES)
            s = lax.dot_general(k_ref[0, :, blk], q_ref[0, :, blk], NT_DIMS, preferred_element_type=F32)
            if masked:
                s = jnp.where(keep, s, MASK_VALUE)
            m_prev = m_scr[h]
            return s, m_prev, jnp.maximum(m_prev, jnp.max(s, axis=0, keepdims=True))

        def accumulate(h, s, m_prev, m_new):
            p = jnp.exp2(s - m_new).astype(BF16)
            acc_scr[h] = jnp.exp2(m_prev - m_new) * acc_scr[h] + _dot(v_ref[0, h * LANES:(h + 1) * LANES, :], p)
            m_scr[h] = m_new

        pending = scores(0)
        for h in range(heads):
            ahead = scores(h + 1) if h + 1 < heads else None
            accumulate(h, *pending)
            pending = ahead

    @pl.when(j < i)
    def _():
        update(False)

    @pl.when(j == i)
    def _():
        update(True)
        row = lax.broadcasted_iota(jnp.int32, (LANES, tq), 0)
        for hp in range(heads // 2):
            a0 = acc_scr[2 * hp]
            a1 = acc_scr[2 * hp + 1]
            o_t = jnp.where(row < dh, a0 / a0[dh:dh + 1, :], a1 / a1[0:1, :])
            o_ref[0, :, hp * LANES:(hp + 1) * LANES] = o_t.T.astype(o_ref.dtype)


def _attn_prompt(qa, ka, va, heads, dh):
    b, t, width = qa.shape
    tq = _largest_divisor(t, 512, SUBLANES)
    nq = t // tq
    qi = np.array([i for i in range(nq) for j in range(i + 1)], np.int32)
    kj = np.array([j for i in range(nq) for j in range(i + 1)], np.int32)
    grid_spec = pltpu.PrefetchScalarGridSpec(
        num_scalar_prefetch=2,
        grid=(b, len(qi)),
        in_specs=[pl.BlockSpec((1, tq, width), lambda bb, s, qi_r, kj_r: (bb, qi_r[s], 0)),
                  pl.BlockSpec((1, tq, width), lambda bb, s, qi_r, kj_r: (bb, kj_r[s], 0)),
                  pl.BlockSpec((1, width, tq), lambda bb, s, qi_r, kj_r: (bb, 0, kj_r[s]))],
        out_specs=pl.BlockSpec((1, tq, heads * dh), lambda bb, s, qi_r, kj_r: (bb, qi_r[s], 0)),
        scratch_shapes=[pltpu.VMEM((heads, 1, tq), F32), pltpu.VMEM((heads, LANES, tq), F32)],
    )
    return pl.pallas_call(
        functools.partial(_attn_kernel, dh=dh),
        grid_spec=grid_spec,
        out_shape=jax.ShapeDtypeStruct((b, t, heads * dh), BF16),
        compiler_params=_params(("parallel", "arbitrary")),
        name="attn_prompt",
    )(jnp.asarray(qi), jnp.asarray(kj), qa, ka, va)


def _attn_sample_kernel(pt_ref, q_ref, kc_ref, vc_ref, lfc_ref, ck_hbm, cv_hbm, clf_hbm, o_ref,
                        kbuf, vbuf, lbuf, sems, *, layer, n_pages, heads, dh):
    b = pl.program_id(0)
    nb = pl.num_programs(0)
    page = lbuf.shape[2]
    hw = heads * dh
    t_cur = q_ref.shape[1]
    rows = heads * t_cur
    npr = n_pages * heads

    def copies(seq, slot):
        out = []
        for p in range(n_pages):
            pid = pt_ref[seq, p]
            lanes = pl.ds(p * page, page)
            out.append(pltpu.make_async_copy(ck_hbm.at[layer, pid], kbuf.at[slot, :, lanes], sems.at[0, slot, p]))
            out.append(pltpu.make_async_copy(cv_hbm.at[layer, pid], vbuf.at[slot, :, lanes], sems.at[1, slot, p]))
            out.append(pltpu.make_async_copy(clf_hbm.at[layer, pid], lbuf.at[slot, pl.ds(p * heads, heads), :],
                                             sems.at[2, slot, p]))
        return out

    @pl.when(b == 0)
    def _():
        for c in copies(0, 0):
            c.start()

    slot = b % 2

    @pl.when(b + 1 < nb)
    def _():
        for c in copies(b + 1, 1 - slot):
            c.start()

    for c in copies(b, slot):
        c.wait()

    ur = lax.broadcasted_iota(jnp.int32, (page, page), 0)
    uc = lax.broadcasted_iota(jnp.int32, (page, page), 1)
    triu = jnp.where(ur <= uc, 1.0, 0.0).astype(BF16)
    l_hi, l_mid, l_lo = _split3(lbuf[slot])
    cum = _dot(l_hi, triu) + _dot(l_mid, triu) + _dot(l_lo, triu)
    mr = lax.broadcasted_iota(jnp.int32, (npr, npr), 0)
    mc = lax.broadcasted_iota(jnp.int32, (npr, npr), 1)
    later = jnp.where((mr % heads == mc % heads) & (mc // heads >= mr // heads), 1.0, 0.0).astype(BF16)
    e = cum - _dot3(later, jnp.broadcast_to(cum[:, page - 1:page], (npr, page)))
    rr = lax.broadcasted_iota(jnp.int32, (n_pages * rows, npr), 0)
    rc = lax.broadcasted_iota(jnp.int32, (n_pages * rows, npr), 1)
    spread = jnp.where((rr // rows == rc // heads) & ((rr % rows) // t_cur == rc % heads), -1.0, 0.0).astype(BF16)
    bias_rows = _dot3(spread, e)
    bias = jnp.concatenate([bias_rows[p * rows:(p + 1) * rows] for p in range(n_pages)], axis=1)

    cc = _dot3(_tri_lower(t_cur), lfc_ref[0])
    rrow = lax.broadcasted_iota(jnp.int32, (rows, hw), 0) // t_cur
    rcol = lax.broadcasted_iota(jnp.int32, (rows, hw), 1) // dh
    q_rep = jnp.concatenate([q_ref[0]] * heads, axis=0)
    qbd = jnp.where(rrow == rcol, q_rep * (dh ** -0.5), 0.0).astype(BF16)
    arow = lax.broadcasted_iota(jnp.int32, (rows, LANES), 0) // t_cur
    acol = lax.broadcasted_iota(jnp.int32, (rows, LANES), 1)
    sel = arow == acol
    neg_sel = jnp.where(sel, -1.0, 0.0).astype(BF16)
    cc_rep = jnp.concatenate([cc] * heads, axis=0)
    dq = jnp.sum(jnp.where(sel, cc_rep, 0.0), axis=-1, keepdims=True)

    nt = lambda a, bm: lax.dot_general(a, bm, NT_DIMS, preferred_element_type=F32)
    s_past = _dot(qbd, kbuf[slot].astype(BF16)) + bias + dq
    c_hi, c_mid, c_lo = _split3(cc)
    s_cur = nt(qbd, kc_ref[0].astype(BF16)) + (nt(neg_sel, c_hi) + nt(neg_sel, c_mid) + nt(neg_sel, c_lo)) + dq
    tq_idx = lax.broadcasted_iota(jnp.int32, (rows, t_cur), 0) % t_cur
    ts_idx = lax.broadcasted_iota(jnp.int32, (rows, t_cur), 1)
    s_cur = jnp.where(ts_idx <= tq_idx, s_cur, MASK_VALUE)

    m = jnp.maximum(jnp.max(s_past, axis=-1, keepdims=True), jnp.max(s_cur, axis=-1, keepdims=True))
    p_past = jnp.exp(s_past - m)
    p_cur = jnp.exp(s_cur - m)
    denom = jnp.sum(p_past, axis=-1, keepdims=True) + jnp.sum(p_cur, axis=-1, keepdims=True)
    o_all = (nt(p_past.astype(BF16), vbuf[slot].astype(BF16))
             + _dot(p_cur.astype(BF16), vc_ref[0].astype(BF16))) / denom
    o_all = jnp.where(rrow == rcol, o_all, 0.0)
    o = o_all[0:t_cur]
    for h in range(1, heads):
        o = o + o_all[h * t_cur:(h + 1) * t_cur]
    o_ref[0] = o.astype(o_ref.dtype)


def _attn_sample(proj, logf, cache_k, cache_v, cache_logf, page_table, layer, cols, heads, dh):
    nb, t_cur, _ = proj.shape
    hw = heads * dh
    depth, n_pool, page = cache_k.shape[:3]
    n_pages = page_table.shape[1]
    ck = jnp.transpose(cache_k, (0, 1, 3, 4, 2)).reshape(depth, n_pool, hw, page)
    cv = jnp.transpose(cache_v, (0, 1, 3, 4, 2)).reshape(depth, n_pool, hw, page)
    clf = jnp.transpose(cache_logf, (0, 1, 3, 2))
    cq, ckc, cvc = cols
    grid_spec = pltpu.PrefetchScalarGridSpec(
        num_scalar_prefetch=1,
        grid=(nb,),
        in_specs=[pl.BlockSpec((1, t_cur, hw), lambda b, pt: (b, 0, cq)),
                  pl.BlockSpec((1, t_cur, hw), lambda b, pt: (b, 0, ckc)),
                  pl.BlockSpec((1, t_cur, hw), lambda b, pt: (b, 0, cvc)),
                  pl.BlockSpec((1, t_cur, LANES), lambda b, pt: (b, 0, 0)),
                  pl.BlockSpec(memory_space=pl.ANY),
                  pl.BlockSpec(memory_space=pl.ANY),
                  pl.BlockSpec(memory_space=pl.ANY)],
        out_specs=pl.BlockSpec((1, t_cur, hw), lambda b, pt: (b, 0, 0)),
        scratch_shapes=[pltpu.VMEM((2, hw, n_pages * page), F32),
                        pltpu.VMEM((2, hw, n_pages * page), F32),
                        pltpu.VMEM((2, n_pages * heads, page), F32),
                        pltpu.SemaphoreType.DMA((3, 2, n_pages))],
    )
    return pl.pallas_call(
        functools.partial(_attn_sample_kernel, layer=layer, n_pages=n_pages, heads=heads, dh=dh),
        grid_spec=grid_spec,
        out_shape=jax.ShapeDtypeStruct((nb, t_cur, hw), BF16),
        compiler_params=_params(("arbitrary",)),
        name="attn_sample",
    )(page_table, proj, proj, proj, logf, ck, cv, clf)


def _hgrn_kernel(q_ref, f_ref, v_ref, og_ref, lb_ref, ng_ref, s0_ref, *rest, chunk, sub):
    y_ref, sout_ref, st_scr = rest[-3:]
    tb, tt, width = q_ref.shape
    heads = s0_ref.shape[2]
    kd = width // heads
    n_chunks = tt // chunk
    n_sub = chunk // sub
    t_idx = pl.program_id(1)
    hb = [slice(h * kd, (h + 1) * kd) for h in range(heads)]
    per_head = lambda fn: jnp.concatenate([fn(h) for h in range(heads)], axis=1)

    @pl.when(t_idx == 0)
    def _():
        for ib in range(tb):
            for h in range(heads):
                st_scr[ib, h] = s0_ref[0, ib, h].T

    lb = lb_ref[...]
    ng = ng_ref[...]
    tri = _tri_lower(chunk)
    sub_col = lax.broadcasted_iota(jnp.int32, (sub, 1), 0)

    def do_chunk(ib, rows):
        g = lb + (1.0 - lb) * _sigmoid(f_ref[ib, rows, :])
        kk = 1.0 - g
        qq = _silu(q_ref[ib, rows, :])
        v = v_ref[ib, rows, :]
        vb = v.astype(BF16)
        bcum = _dot3(tri, jnp.log(g))
        st = [st_scr[ib, h] for h in range(heads)]
        qe = (qq * jnp.exp(bcum)).astype(BF16)
        nt = lambda x, y: lax.dot_general(x, y, NT_DIMS, preferred_element_type=F32)
        o_inter = [nt(qe[:, hb[h]], st[h].astype(BF16)) for h in range(heads)]
        b_last = bcum[chunk - 1:chunk]
        kdec = (kk * jnp.exp(b_last - bcum)).astype(BF16)
        e_last = jnp.exp(b_last)
        st_add = [lax.dot_general(vb[:, hb[h]], kdec[:, hb[h]], TN_DIMS, preferred_element_type=F32)
                  for h in range(heads)]
        cross_sc = {}
        for si in range(1, n_sub):
            lo_r, hi_r = si * sub, (si + 1) * sub
            ref = bcum[lo_r - 1:lo_r]
            a = (qq[lo_r:hi_r] * jnp.exp(bcum[lo_r:hi_r] - ref)).astype(BF16)
            bm = (kk[:lo_r] * jnp.exp(ref - bcum[:lo_r])).astype(BF16)
            cross_sc[si] = [nt(a[:, hb[h]], bm[:, hb[h]]) for h in range(heads)]
        diag = []
        for si in range(n_sub):
            lo_r, hi_r = si * sub, (si + 1) * sub
            qs, ks, vs, bs2 = qq[lo_r:hi_r], kk[lo_r:hi_r], v[lo_r:hi_r], bcum[lo_r:hi_r] * LOG2_E
            acc = None
            for s in range(sub):
                y = qs * ks[s:s + 1] * jnp.exp2(bs2 - bs2[s:s + 1])
                term = per_head(lambda h: jnp.where(sub_col >= s, jnp.sum(y[:, hb[h]], axis=-1, keepdims=True),
                                                    0.0) * vs[s:s + 1, hb[h]])
                acc = term if acc is None else acc + term
            diag.append(acc)
        pieces = []
        for si in range(n_sub):
            lo_r, hi_r = si * sub, (si + 1) * sub
            acc = diag[si] + per_head(lambda h: o_inter[h][lo_r:hi_r])
            if si > 0:
                acc = acc + per_head(lambda h: _dot(cross_sc[si][h].astype(BF16), vb[:lo_r, hb[h]]))
            pieces.append(acc)
        o = jnp.concatenate(pieces, axis=0) if n_sub > 1 else pieces[0]
        for h in range(heads):
            st_scr[ib, h] = e_last[:, hb[h]] * st[h] + st_add[h]
        o = per_head(lambda h: o[:, hb[h]] * lax.rsqrt(jnp.mean(o[:, hb[h]] * o[:, hb[h]], axis=-1, keepdims=True)
                                                       + RMS_EPS))
        y_ref[ib, rows, :] = (o * ng * _silu(og_ref[ib, rows, :])).astype(y_ref.dtype)

    for ib in range(tb):
        if n_chunks == 1:
            do_chunk(ib, slice(0, chunk))
        else:
            def body(c, carry, ib=ib):
                do_chunk(ib, pl.ds(pl.multiple_of(c * chunk, chunk), chunk))
                return carry
            lax.fori_loop(0, n_chunks, body, 0)

    @pl.when(t_idx == pl.num_programs(1) - 1)
    def _():
        for ib in range(tb):
            for h in range(heads):
                sout_ref[0, ib, h] = st_scr[ib, h].T


def _hgrn(proj, s0, l_in, lb, norm_g, cols, layer, depth, stacked):
    nb, g, _ = proj.shape
    hg_heads, kd, vd = s0.shape[2:]
    width = hg_heads * kd
    extra = (stacked,)
    if g >= 64:
        tb, tt = 1, _largest_divisor(g, 512, 64)
        chunk, sub = 64, 8
    else:
        tb, tt = _largest_divisor(nb, 8, 1), g
        chunk, sub = g, g
    col = lambda c0: pl.BlockSpec((tb, tt, width), lambda i, t: (i, t, c0))
    return pl.pallas_call(
        functools.partial(_hgrn_kernel, chunk=chunk, sub=sub),
        grid=(nb // tb, g // tt),
        in_specs=[col(cols[0]), col(cols[1]), col(cols[2]), col(cols[3]),
                  pl.BlockSpec((1, width), lambda i, t: (0, 0)),
                  pl.BlockSpec((1, width), lambda i, t: (0, 0)),
                  pl.BlockSpec((1, tb, hg_heads, kd, vd), lambda i, t: (l_in, i, 0, 0, 0))]
                 + [pl.BlockSpec(memory_space=pl.ANY)] * len(extra),
        out_specs=[pl.BlockSpec((tb, tt, width), lambda i, t: (i, t, 0)),
                   pl.BlockSpec((1, tb, hg_heads, kd, vd), lambda i, t: (layer, i, 0, 0, 0))],
        out_shape=[jax.ShapeDtypeStruct((nb, g, width), BF16),
                   jax.ShapeDtypeStruct((depth, nb, hg_heads, kd, vd), F32)],
        input_output_aliases={7 + i: 1 for i in range(len(extra))},
        scratch_shapes=[pltpu.VMEM((tb, hg_heads, vd, kd), F32)],
        compiler_params=_params(("parallel", "arbitrary")),
        name="hgrn",
    )(proj, proj, proj, proj, lb.reshape(1, -1), norm_g.reshape(1, -1), s0, *extra)


def _merge_kernel(ya_ref, yb_ref, ga_ref, gb_ref, x_ref, gm_ref, scf_ref, shf_ref,
                  wa_ref, wb_ref, wo_ref, lg_ref, lbias_ref, x1_ref, h2_ref, *, alpha):
    tb, tg, d = x_ref.shape
    tm = tb * tg
    y_a = _dot(ya_ref[...].reshape(tm, -1), wa_ref[...])
    y_b = _dot(yb_ref[...].reshape(tm, -1), wb_ref[...])
    merged = _sigmoid(ga_ref[...].reshape(tm, d)) * y_a + _sigmoid(gb_ref[...].reshape(tm, d)) * y_b
    z = _dot(merged.astype(BF16), wo_ref[...]).reshape(tb, tg, d)
    x1 = _layer_norm(alpha * x_ref[...] + (1.0 + gm_ref[...]) * z, lg_ref[...], lbias_ref[...])
    x1_ref[...] = x1
    h2_ref[...] = (x1 * (1.0 + scf_ref[...]) + shf_ref[...]).astype(h2_ref.dtype)


def _merge(attn, yb, proj, x3, ada3, w_br_a, w_br_b, w_o, ln_g, ln_b, alpha, gate_cols, rows=512):
    nb, g, d = x3.shape
    tb, tg = _row_tiles(nb, g, rows)
    ca, cb = gate_cols
    wa = attn.shape[-1]
    wb = yb.shape[-1]
    tok = lambda w, c: pl.BlockSpec((tb, tg, w), lambda i, t: (i, t, c))
    mod = lambda c: pl.BlockSpec((tb, 1, d), lambda i, t: (i, 0, c))
    full = lambda shape: pl.BlockSpec(shape, lambda i, t: (0,) * len(shape))
    return pl.pallas_call(
        functools.partial(_merge_kernel, alpha=alpha),
        grid=(nb // tb, g // tg),
        in_specs=[tok(wa, 0), tok(wb, 0), tok(d, ca), tok(d, cb), tok(d, 0),
                  mod(2), mod(4), mod(3),
                  full(w_br_a.shape), full(w_br_b.shape), full(w_o.shape), full((1, d)), full((1, d))],
        out_specs=[tok(d, 0), tok(d, 0)],
        out_shape=[jax.ShapeDtypeStruct((nb, g, d), F32), jax.ShapeDtypeStruct((nb, g, d), BF16)],
        compiler_params=_params(("parallel", "parallel")),
        name="merge",
    )(attn, yb, proj, proj, x3, ada3, ada3, ada3, w_br_a, w_br_b, w_o, ln_g.reshape(1, d), ln_b.reshape(1, d))


def _gelu(x):
    return 0.5 * x * (1.0 + lax.erf(x * (2.0 ** -0.5)))


def _ffn_kernel(h_ref, wu_ref, cw_ref, cb_ref, wd_ref, prev_ref, x1_ref, gf_ref, lg_ref, lbias_ref,
                x2_ref, cnew_ref, act_scr, carry_scr, *, alpha, tf):
    tb, tg, d = h_ref.shape
    tm = tb * tg
    dff = wd_ref.shape[0]
    first = pl.program_id(1) == 0
    h = h_ref[...].reshape(tm, d)

    @pl.when(first)
    def _():
        carry_scr[...] = jnp.zeros_like(carry_scr)

    tt = lax.broadcasted_iota(jnp.int32, (tb, tg, tf), 1)
    for c in range(dff // tf):
        cols = slice(c * tf, (c + 1) * tf)
        a = _dot(h, wu_ref[:, cols])
        gate = _dot(h, wu_ref[:, dff + c * tf:dff + (c + 1) * tf])
        a3 = a.reshape(tb, tg, tf)
        tail = carry_scr[:, :, cols]
        p0 = jnp.where(first, prev_ref[:, 0:1, cols], tail[:, SUBLANES - 2:SUBLANES - 1, :])
        p1 = jnp.where(first, prev_ref[:, 1:2, cols], tail[:, SUBLANES - 1:SUBLANES, :])
        am1 = jnp.where(tt == 0, p1, pltpu.roll(a, 1, 0).reshape(tb, tg, tf))
        am2 = jnp.where(tt == 0, p0, jnp.where(tt == 1, p1, pltpu.roll(a, 2, 0).reshape(tb, tg, tf)))
        conv = cb_ref[:, cols] + cw_ref[0:1, cols] * am2 + cw_ref[1:2, cols] * am1 + cw_ref[2:3, cols] * a3
        act_scr[:, cols] = (_gelu(conv) * gate.reshape(tb, tg, tf)).astype(BF16).reshape(tm, tf)
        carry_scr[:, :, cols] = a3[:, tg - SUBLANES:, :]
        cnew_ref[:, 0, :, cols] = a3[:, tg - 2:, :]

    y = _dot(act_scr[...], wd_ref[...]).reshape(tb, tg, d)
    x2_ref[...] = _layer_norm(alpha * x1_ref[...] + (1.0 + gf_ref[...]) * y, lg_ref[...], lbias_ref[...])


def _ffn(h2, x1, ada3, prev, w_up, conv_w, conv_b, w_down, ln_g, ln_b, alpha, rows=512, tf=256):
    nb, g, d = x1.shape
    dff = w_down.shape[0]
    tb, tg = _row_tiles(nb, g, rows)
    resident = lambda shape: pl.BlockSpec(shape, lambda i, t: (0,) * len(shape), pipeline_mode=pl.Buffered(1))
    return pl.pallas_call(
        functools.partial(_ffn_kernel, alpha=alpha, tf=tf),
        grid=(nb // tb, g // tg),
        in_specs=[pl.BlockSpec((tb, tg, d), lambda i, t: (i, t, 0)),
                  resident(w_up.shape),
                  resident(conv_w.shape),
                  resident((1, dff)),
                  resident(w_down.shape),
                  pl.BlockSpec((tb, 2, dff), lambda i, t: (i, 0, 0)),
                  pl.BlockSpec((tb, tg, d), lambda i, t: (i, t, 0)),
                  pl.BlockSpec((tb, 1, d), lambda i, t: (i, 0, 5)),
                  resident((1, d)),
                  resident((1, d))],
        out_specs=[pl.BlockSpec((tb, tg, d), lambda i, t: (i, t, 0)),
                   pl.BlockSpec((tb, 1, 2, dff), lambda i, t: (i, t, 0, 0))],
        out_shape=[jax.ShapeDtypeStruct((nb, g, d), F32), jax.ShapeDtypeStruct((nb, g // tg, 2, dff), F32)],
        scratch_shapes=[pltpu.VMEM((tb * tg, dff), BF16), pltpu.VMEM((tb, SUBLANES, dff), F32)],
        compiler_params=_params(("parallel", "arbitrary")),
        name="ffn",
    )(h2, w_up, conv_w, conv_b.reshape(1, dff), w_down, prev, x1, ada3,
      ln_g.reshape(1, d), ln_b.reshape(1, d))


def kernel(x_prompt, x_sample, cache_k, cache_v, cache_logf, state_hgrn, state_conv, page_table, c_prompt, c_sample, ln0_g, ln0_b, w_in, fox_f_bias, hg_lb, hg_norm_g, w_br_a, w_br_b, w_o, ln1_g, ln1_b, w_up, conv_w, conv_b, w_down, ln2_g, ln2_b, w_ada, b_ada):
    bp, tp, d = x_prompt.shape
    bs, ts, _ = x_sample.shape
    depth = w_in.shape[0]
    heads, dh = cache_k.shape[3], cache_k.shape[4]
    hw = heads * dh
    hg_heads, kd, vd = state_hgrn.shape[2:]
    hgw = hg_heads * kd
    dff = w_down.shape[1]
    assert conv_w.shape[1] == 3 and state_conv.shape[2] == 2
    assert kd == LANES and vd == LANES and 2 * dh == LANES and hw == hgw == hg_heads * vd and d == 2 * hw
    alpha = (2 * depth) ** 0.25

    lb_soft = jax.nn.softmax(hg_lb.astype(F32), axis=0)
    lb_all = jnp.cumsum(lb_soft, axis=0) - lb_soft[0]

    o_f = 3 * hw
    o_b = o_f + heads
    o_g = o_b + 4 * hgw
    w_main = jnp.concatenate([w_in[:, :, o_g:], w_in[:, :, o_b:o_g], w_in[:, :, :o_f]], axis=2).astype(BF16)
    w_f = jnp.pad(w_in[:, :, o_f:o_b], ((0, 0), (0, 0), (0, LANES - heads))).astype(BF16)
    f_bias = jnp.pad(fox_f_bias, ((0, 0), (0, LANES - heads))).reshape(depth, 1, LANES)
    w_rest = 2 * d + 4 * hgw
    gate_cols = (0, 1)
    hg_cols = tuple(2 * d // hgw + i for i in range(4))
    qkv_cols = tuple(w_rest // hw + i for i in range(3))
    k_off, v_off = w_rest + hw, w_rest + 2 * hw
    w_br_a16, w_br_b16, w_o16 = w_br_a.astype(BF16), w_br_b.astype(BF16), w_o.astype(BF16)
    w_up16, w_down16 = w_up.astype(BF16), w_down.astype(BF16)
    consts = _aug_constants(heads, dh)

    ada = _ada(jnp.concatenate([c_prompt, c_sample], axis=0), w_ada, b_ada)
    xp = _ln0(x_prompt.reshape(bp * tp, d), ln0_g, ln0_b).reshape(bp, tp, d)
    xs = _ln0(x_sample.reshape(bs * ts, d), ln0_g, ln0_b).reshape(bs, ts, d)
    s0_p = jnp.zeros((1, bp) + state_hgrn.shape[2:], F32)
    conv0_p = jnp.zeros((bp, 2, dff), F32)

    kv_t = [jnp.zeros((depth, bp, hw, tp), F32), jnp.zeros((depth, bp, hw, tp), F32)]
    hg_p = jnp.zeros((depth, bp) + state_hgrn.shape[2:], F32)
    hg_s = jnp.zeros((depth, bs) + state_hgrn.shape[2:], F32)
    outs_p, outs_s = [], []
    for l in range(depth):
        ada_p = ada[l, :bp].reshape(bp, 1, 6 * d)
        ada_s = ada[l, bp:].reshape(bs, 1, 6 * d)

        proj, logf = _inproj(xp, ada_p, w_main[l], w_f[l], f_bias[l], width=w_rest, tn=d)
        qa, ka, va_t, *kv_t = _attn_prep(xp, ada_p, w_main[l], logf, consts, qkv_cols, heads, dh, l, depth, kv_t)
        attn = _attn_prompt(qa, ka, va_t, heads, dh)
        yb, hg_p = _hgrn(proj, s0_p, 0, lb_all[l], hg_norm_g[l], hg_cols, l, depth, hg_p)
        x1, h2 = _merge(attn, yb, proj, xp, ada_p, w_br_a16[l], w_br_b16[l], w_o16[l], ln1_g[l], ln1_b[l],
                        alpha, gate_cols)
        xp, conv_new = _ffn(h2, x1, ada_p, conv0_p, w_up16[l], conv_w[l], conv_b[l], w_down16[l],
                            ln2_g[l], ln2_b[l], alpha)
        outs_p.append((logf[:, :, :heads], conv_new[:, -1]))

        proj, logf = _inproj(xs, ada_s, w_main[l], w_f[l], f_bias[l], width=w_main.shape[2],
                             tn=_largest_divisor(w_main.shape[2], 2048, LANES))
        attn = _attn_sample(proj, logf, cache_k, cache_v, cache_logf, page_table, l, qkv_cols, heads, dh)
        yb, hg_s = _hgrn(proj, state_hgrn, l, lb_all[l], hg_norm_g[l], hg_cols, l, depth, hg_s)
        x1, h2 = _merge(attn, yb, proj, xs, ada_s, w_br_a16[l], w_br_b16[l], w_o16[l], ln1_g[l], ln1_b[l],
                        alpha, gate_cols)
        xs, conv_new = _ffn(h2, x1, ada_s, state_conv[l], w_up16[l], conv_w[l], conv_b[l], w_down16[l],
                            ln2_g[l], ln2_b[l], alpha)
        outs_s.append((proj[:, :, k_off:k_off + hw].reshape(bs, ts, heads, dh),
                       proj[:, :, v_off:v_off + hw].reshape(bs, ts, heads, dh),
                       logf[:, :, :heads], conv_new[:, -1]))

    stack = lambda outs, i: jnp.stack([o[i] for o in outs])
    k_p, v_p = (jnp.transpose(a.reshape(depth, bp, heads, dh, tp), (0, 1, 4, 2, 3)) for a in kv_t)
    return (xp, xs, k_p, v_p, stack(outs_p, 0), hg_p, stack(outs_p, 1),
            stack(outs_s, 0), stack(outs_s, 1), stack(outs_s, 2), hg_s, stack(outs_s, 3))
```

```python
import functools

import numpy as np
import jax
import jax.numpy as jnp
from jax import lax
from jax.experimental import pallas as pl
from jax.experimental.pallas import tpu as pltpu

F32 = jnp.float32
BF16 = jnp.bfloat16

LN_EPS = 1e-5
RMS_EPS = 1e-6
MASK_VALUE = -1e30
LOG2_E = 1.4426950408889634
LANES = 128
SUBLANES = 8
VMEM_LIMIT = 56 * 1024 * 1024

NT_DIMS = (((1,), (1,)), ((), ()))
TN_DIMS = (((0,), (0,)), ((), ()))


def _params(sem):
    return pltpu.CompilerParams(dimension_semantics=sem, vmem_limit_bytes=VMEM_LIMIT)


def _largest_divisor(n, target, mult):
    best = None
    for d in range(1, n + 1):
        if n % d == 0 and d <= target and d % mult == 0:
            best = d
    return n if best is None else best


def _row_tiles(nb, g, target):
    if g >= target:
        return 1, _largest_divisor(g, target, SUBLANES)
    return _largest_divisor(nb, max(1, target // g), 1), g


def _split3(x):
    hi = x.astype(BF16)
    r1 = x - hi.astype(F32)
    mid = r1.astype(BF16)
    lo = (r1 - mid.astype(F32)).astype(BF16)
    return hi, mid, lo


def _dot(a, b):
    return jnp.dot(a, b, preferred_element_type=F32)


def _dot3(a_bf16, x_f32):
    hi, mid, lo = _split3(x_f32)
    return _dot(a_bf16, hi) + _dot(a_bf16, mid) + _dot(a_bf16, lo)


def _layer_norm(x, g, b):
    mu = jnp.mean(x, axis=-1, keepdims=True)
    xc = x - mu
    var = jnp.mean(xc * xc, axis=-1, keepdims=True)
    return xc * lax.rsqrt(var + LN_EPS) * g + b


def _sigmoid(x):
    return 1.0 / (1.0 + jnp.exp(-x))


def _silu(x):
    return x * _sigmoid(x)


def _log_sigmoid(x):
    return jnp.minimum(x, 0.0) - jnp.log1p(jnp.exp(-jnp.abs(x)))


def _tri_lower(n):
    r = lax.broadcasted_iota(jnp.int32, (n, n), 0)
    c = lax.broadcasted_iota(jnp.int32, (n, n), 1)
    return jnp.where(r >= c, 1.0, 0.0).astype(BF16)


def _ln0_kernel(x_ref, g_ref, b_ref, o_ref):
    o_ref[...] = _layer_norm(x_ref[...], g_ref[...], b_ref[...])


def _ln0(x2, g, b):
    n, d = x2.shape
    tm = _largest_divisor(n, 1024, SUBLANES)
    return pl.pallas_call(
        _ln0_kernel,
        grid=(n // tm,),
        in_specs=[pl.BlockSpec((tm, d), lambda i: (i, 0)),
                  pl.BlockSpec((1, d), lambda i: (0, 0)),
                  pl.BlockSpec((1, d), lambda i: (0, 0))],
        out_specs=pl.BlockSpec((tm, d), lambda i: (i, 0)),
        out_shape=jax.ShapeDtypeStruct((n, d), F32),
        compiler_params=_params(("parallel",)),
        name="ln0",
    )(x2, g.reshape(1, d), b.reshape(1, d))


def _ada_kernel(c_ref, w_ref, b_ref, o_ref):
    s = _silu(c_ref[...]).astype(BF16)
    o_ref[0] = _dot(s, w_ref[0].astype(BF16)) + b_ref[0]


def _ada(c_all, w_ada, b_ada):
    depth, d, n6 = w_ada.shape
    nc = c_all.shape[0]
    tn = _largest_divisor(n6, 1536, LANES)
    return pl.pallas_call(
        _ada_kernel,
        grid=(depth, n6 // tn),
        in_specs=[pl.BlockSpec((nc, d), lambda l, j: (0, 0)),
                  pl.BlockSpec((1, d, tn), lambda l, j: (l, 0, j)),
                  pl.BlockSpec((1, 1, tn), lambda l, j: (l, 0, j))],
        out_specs=pl.BlockSpec((1, nc, tn), lambda l, j: (l, 0, j)),
        out_shape=jax.ShapeDtypeStruct((depth, nc, n6), F32),
        compiler_params=_params(("parallel", "parallel")),
        name="ada",
    )(c_all, w_ada, b_ada.reshape(depth, 1, n6))


def _inproj_kernel(x_ref, sh_ref, sc_ref, w_ref, wf_ref, fb_ref, proj_ref, logf_ref, *, tn):
    tb, tg, d = x_ref.shape
    h = (x_ref[...] * (1.0 + sc_ref[...]) + sh_ref[...]).reshape(tb * tg, d).astype(BF16)
    f = _dot(h, wf_ref[...]) + fb_ref[...]
    logf_ref[...] = _log_sigmoid(f).reshape(tb, tg, LANES)
    for c in range(w_ref.shape[1] // tn):
        cols = slice(c * tn, (c + 1) * tn)
        proj_ref[:, :, cols] = _dot(h, w_ref[:, cols]).reshape(tb, tg, tn)


def _inproj(x3, ada3, w_main, w_f, f_bias, width, tn=512, rows=512):
    nb, g, d = x3.shape
    tb, tg = _row_tiles(nb, g, rows)
    resident = lambda shape: pl.BlockSpec(shape, lambda i, t: (0,) * len(shape), pipeline_mode=pl.Buffered(1))
    return pl.pallas_call(
        functools.partial(_inproj_kernel, tn=tn),
        grid=(nb // tb, g // tg),
        in_specs=[pl.BlockSpec((tb, tg, d), lambda i, t: (i, t, 0)),
                  pl.BlockSpec((tb, 1, d), lambda i, t: (i, 0, 0)),
                  pl.BlockSpec((tb, 1, d), lambda i, t: (i, 0, 1)),
                  resident((d, width)),
                  resident((d, LANES)),
                  resident((1, LANES))],
        out_specs=[pl.BlockSpec((tb, tg, width), lambda i, t: (i, t, 0)),
                   pl.BlockSpec((tb, tg, LANES), lambda i, t: (i, t, 0))],
        out_shape=[jax.ShapeDtypeStruct((nb, g, width), F32),
                   jax.ShapeDtypeStruct((nb, g, LANES), F32)],
        compiler_params=_params(("parallel", "parallel")),
        name="inproj",
    )(x3, ada3, ada3, w_main, w_f, f_bias)


def _aug_constants(heads, dh):
    width = heads * LANES
    pq = np.zeros((3, LANES, width), np.float32)
    pk = np.zeros((3, LANES, width), np.float32)
    ones_q = np.zeros((1, width), np.float32)
    ones_k = np.zeros((1, width), np.float32)
    own = np.zeros((1, width), np.float32)
    for h in range(heads):
        o_h = (h % 2) * dh
        a_h = h * LANES + (dh - o_h)
        own[0, h * LANES + o_h:h * LANES + o_h + dh] = 1.0
        for i in range(3):
            pq[i, h, a_h + i] = 1.0
            pk[i, h, a_h + 3 + i] = 1.0
            ones_q[0, a_h + 3 + i] = 1.0
            ones_k[0, a_h + i] = 1.0
    return (jnp.asarray(pq, BF16), jnp.asarray(pk, BF16), jnp.asarray(ones_q), jnp.asarray(ones_k),
            jnp.asarray(own))


def _prep_kernel(x_ref, sh_ref, sc_ref, wq_ref, wk_ref, wv_ref, lf_ref, pq_ref, pk_ref, oq_ref, ok_ref, own_ref,
                 *rest, scale):
    qa_ref, ka_ref, va_ref, kt_ref, vt_ref, carry = rest[-6:]
    tc = x_ref.shape[1]
    heads = qa_ref.shape[2] // LANES

    @pl.when(pl.program_id(1) == 0)
    def _():
        carry[...] = jnp.zeros_like(carry)

    h = (x_ref[0] * (1.0 + sc_ref[0]) + sh_ref[0]).astype(BF16)
    q = _dot(h, wq_ref[...])
    k = _dot(h, wk_ref[...])
    v = _dot(h, wv_ref[...])

    d = _dot3(_tri_lower(tc), lf_ref[0]) + carry[...]
    carry[...] = d[tc - 1:tc, :]
    hi, mid, lo = _split3(d * LOG2_E)
    aux_q = _dot(hi, pq_ref[0]) + _dot(mid, pq_ref[1]) + _dot(lo, pq_ref[2]) + oq_ref[...]
    aux_k = ok_ref[...] - (_dot(hi, pk_ref[0]) + _dot(mid, pk_ref[1]) + _dot(lo, pk_ref[2]))
    own = own_ref[...] > 0.5

    def widen(x):
        return jnp.concatenate([x[:, (h // 2) * LANES:(h // 2 + 1) * LANES] for h in range(heads)], axis=1)

    qa_ref[0] = jnp.where(own, widen(q) * (scale * LOG2_E), aux_q).astype(BF16)
    ka_ref[0] = jnp.where(own, widen(k), aux_k).astype(BF16)
    dh = LANES // 2
    row = lax.broadcasted_iota(jnp.int32, (LANES, tc), 0)
    for hp in range(heads // 2):
        pair = slice(hp * LANES, (hp + 1) * LANES)
        kt_ref[0, 0, pair, :] = k[:, pair].T
        vt = v[:, pair].T
        vt_ref[0, 0, pair, :] = vt
        va_ref[0, (2 * hp) * LANES:(2 * hp + 1) * LANES, :] = jnp.where(
            row < dh, vt, jnp.where(row == dh, 1.0, 0.0)).astype(BF16)
        va_ref[0, (2 * hp + 1) * LANES:(2 * hp + 2) * LANES, :] = jnp.where(
            row >= dh, vt, jnp.where(row == 0, 1.0, 0.0)).astype(BF16)


def _attn_prep(x3, ada3, w_main, logf, consts, cols, heads, dh, layer, depth, stacked):
    b, t, d = x3.shape
    hw = heads * dh
    width = heads * LANES
    tc = _largest_divisor(t, 512, SUBLANES)
    pq, pk, oq, ok, own = consts
    cq, ck, cv = cols
    full = lambda shape: pl.BlockSpec(shape, lambda i, j: (0,) * len(shape))
    out = jax.ShapeDtypeStruct((b, t, width), BF16)
    cache = jax.ShapeDtypeStruct((depth, b, hw, t), F32)
    cache_spec = pl.BlockSpec((1, 1, hw, tc), lambda i, j: (layer, i, 0, j))
    extra = tuple(stacked)
    n_in = 12
    return pl.pallas_call(
        functools.partial(_prep_kernel, scale=dh ** -0.5),
        grid=(b, t // tc),
        in_specs=[pl.BlockSpec((1, tc, d), lambda i, j: (i, j, 0)),
                  pl.BlockSpec((1, 1, d), lambda i, j: (i, 0, 0)),
                  pl.BlockSpec((1, 1, d), lambda i, j: (i, 0, 1)),
                  pl.BlockSpec((d, hw), lambda i, j: (0, cq)),
                  pl.BlockSpec((d, hw), lambda i, j: (0, ck)),
                  pl.BlockSpec((d, hw), lambda i, j: (0, cv)),
                  pl.BlockSpec((1, tc, LANES), lambda i, j: (i, j, 0)),
                  full(pq.shape), full(pk.shape), full(oq.shape), full(ok.shape), full(own.shape)]
                 + [pl.BlockSpec(memory_space=pl.ANY)] * len(extra),
        out_specs=[pl.BlockSpec((1, tc, width), lambda i, j: (i, j, 0)),
                   pl.BlockSpec((1, tc, width), lambda i, j: (i, j, 0)),
                   pl.BlockSpec((1, width, tc), lambda i, j: (i, 0, j)),
                   cache_spec, cache_spec],
        out_shape=[out, out, jax.ShapeDtypeStruct((b, width, t), BF16), cache, cache],
        input_output_aliases={n_in + i: 3 + i for i in range(len(extra))},
        scratch_shapes=[pltpu.VMEM((1, LANES), F32)],
        compiler_params=_params(("parallel", "arbitrary")),
        name="attn_prep",
    )(x3, ada3, ada3, w_main, w_main, w_main, logf, pq, pk, oq, ok, own, *extra)


def _attn_kernel(qi_ref, kj_ref, q_ref, k_ref, v_ref, o_ref, m_scr, acc_scr, *, dh):
    tq = q_ref.shape[1]
    tk = k_ref.shape[1]
    heads = q_ref.shape[2] // LANES
    step = pl.program_id(1)
    i = qi_ref[step]
    j = kj_ref[step]

    @pl.when(j == 0)
    def _():
        m_scr[...] = jnp.full_like(m_scr, MASK_VALUE)
        acc_scr[...] = jnp.zeros_like(acc_scr)

    def update(masked):
        if masked:
            key = lax.broadcasted_iota(jnp.int32, (tk, tq), 0)
            qry = lax.broadcasted_iota(jnp.int32, (tk, tq), 1)
            keep = key <= qry

        def scores(h):
            blk = slice(h * LANES, (h + 1) * LANES)
            s = lax.dot_general(k_ref[0, :, blk], q_ref[0, :, blk], NT_DIMS, preferred_element_type=F32)
            if masked:
                s = jnp.where(keep, s, MASK_VALUE)
            m_prev = m_scr[h]
            return s, m_prev, jnp.maximum(m_prev, jnp.max(s, axis=0, keepdims=True))

        def accumulate(h, s, m_prev, m_new):
            p = jnp.exp2(s - m_new).astype(BF16)
            acc_scr[h] = jnp.exp2(m_prev - m_new) * acc_scr[h] + _dot(v_ref[0, h * LANES:(h + 1) * LANES, :], p)
            m_scr[h] = m_new

        ahead = 2
        pending = [scores(h) for h in range(ahead)]
        for h in range(heads):
            if h + ahead < heads:
                pending.append(scores(h + ahead))
            accumulate(h, *pending.pop(0))

    @pl.when(j < i)
    def _():
        update(False)

    @pl.when(j == i)
    def _():
        update(True)
        row = lax.broadcasted_iota(jnp.int32, (LANES, tq), 0)
        for hp in range(heads // 2):
            a0 = acc_scr[2 * hp]
            a1 = acc_scr[2 * hp + 1]
            o_t = jnp.where(row < dh, a0 / a0[dh:dh + 1, :], a1 / a1[0:1, :])
            o_ref[0, :, hp * LANES:(hp + 1) * LANES] = o_t.T.astype(o_ref.dtype)


def _attn_prompt(qa, ka, va, heads, dh):
    b, t, width = qa.shape
    tq = _largest_divisor(t, 512, SUBLANES)
    nq = t // tq
    qi = np.array([i for i in range(nq) for j in range(i + 1)], np.int32)
    kj = np.array([j for i in range(nq) for j in range(i + 1)], np.int32)
    grid_spec = pltpu.PrefetchScalarGridSpec(
        num_scalar_prefetch=2,
        grid=(b, len(qi)),
        in_specs=[pl.BlockSpec((1, tq, width), lambda bb, s, qi_r, kj_r: (bb, qi_r[s], 0)),
                  pl.BlockSpec((1, tq, width), lambda bb, s, qi_r, kj_r: (bb, kj_r[s], 0)),
                  pl.BlockSpec((1, width, tq), lambda bb, s, qi_r, kj_r: (bb, 0, kj_r[s]))],
        out_specs=pl.BlockSpec((1, tq, heads * dh), lambda bb, s, qi_r, kj_r: (bb, qi_r[s], 0)),
        scratch_shapes=[pltpu.VMEM((heads, 1, tq), F32), pltpu.VMEM((heads, LANES, tq), F32)],
    )
    return pl.pallas_call(
        functools.partial(_attn_kernel, dh=dh),
        grid_spec=grid_spec,
        out_shape=jax.ShapeDtypeStruct((b, t, heads * dh), BF16),
        compiler_params=_params(("parallel", "arbitrary")),
        name="attn_prompt",
    )(jnp.asarray(qi), jnp.asarray(kj), qa, ka, va)


def _attn_sample_kernel(pt_ref, q_ref, kc_ref, vc_ref, lfc_ref, ck_hbm, cv_hbm, clf_hbm, o_ref,
                        kbuf, vbuf, lbuf, sems, *, layer, n_pages, heads, dh):
    b = pl.program_id(0)
    nb = pl.num_programs(0)
    page = lbuf.shape[2]
    hw = heads * dh
    t_cur = q_ref.shape[1]
    rows = heads * t_cur
    npr = n_pages * heads

    def copies(seq, slot):
        out = []
        for p in range(n_pages):
            pid = pt_ref[seq, p]
            lanes = pl.ds(p * page, page)
            out.append(pltpu.make_async_copy(ck_hbm.at[layer, pid], kbuf.at[slot, :, lanes], sems.at[0, slot, p]))
            out.append(pltpu.make_async_copy(cv_hbm.at[layer, pid], vbuf.at[slot, :, lanes], sems.at[1, slot, p]))
            out.append(pltpu.make_async_copy(clf_hbm.at[layer, pid], lbuf.at[slot, pl.ds(p * heads, heads), :],
                                             sems.at[2, slot, p]))
        return out

    @pl.when(b == 0)
    def _():
        for c in copies(0, 0):
            c.start()

    slot = b % 2

    @pl.when(b + 1 < nb)
    def _():
        for c in copies(b + 1, 1 - slot):
            c.start()

    for c in copies(b, slot):
        c.wait()

    ur = lax.broadcasted_iota(jnp.int32, (page, page), 0)
    uc = lax.broadcasted_iota(jnp.int32, (page, page), 1)
    triu = jnp.where(ur <= uc, 1.0, 0.0).astype(BF16)
    l_hi, l_mid, l_lo = _split3(lbuf[slot])
    cum = _dot(l_hi, triu) + _dot(l_mid, triu) + _dot(l_lo, triu)
    mr = lax.broadcasted_iota(jnp.int32, (npr, npr), 0)
    mc = lax.broadcasted_iota(jnp.int32, (npr, npr), 1)
    later = jnp.where((mr % heads == mc % heads) & (mc // heads >= mr // heads), 1.0, 0.0).astype(BF16)
    e = cum - _dot3(later, jnp.broadcast_to(cum[:, page - 1:page], (npr, page)))
    rr = lax.broadcasted_iota(jnp.int32, (n_pages * rows, npr), 0)
    rc = lax.broadcasted_iota(jnp.int32, (n_pages * rows, npr), 1)
    spread = jnp.where((rr // rows == rc // heads) & ((rr % rows) // t_cur == rc % heads), -1.0, 0.0).astype(BF16)
    bias_rows = _dot3(spread, e)
    bias = jnp.concatenate([bias_rows[p * rows:(p + 1) * rows] for p in range(n_pages)], axis=1)

    cc = _dot3(_tri_lower(t_cur), lfc_ref[0])
    rrow = lax.broadcasted_iota(jnp.int32, (rows, hw), 0) // t_cur
    rcol = lax.broadcasted_iota(jnp.int32, (rows, hw), 1) // dh
    q_rep = jnp.concatenate([q_ref[0]] * heads, axis=0)
    qbd = jnp.where(rrow == rcol, q_rep * (dh ** -0.5), 0.0).astype(BF16)
    arow = lax.broadcasted_iota(jnp.int32, (rows, LANES), 0) // t_cur
    acol = lax.broadcasted_iota(jnp.int32, (rows, LANES), 1)
    sel = arow == acol
    neg_sel = jnp.where(sel, -1.0, 0.0).astype(BF16)
    cc_rep = jnp.concatenate([cc] * heads, axis=0)
    dq = jnp.sum(jnp.where(sel, cc_rep, 0.0), axis=-1, keepdims=True)

    nt = lambda a, bm: lax.dot_general(a, bm, NT_DIMS, preferred_element_type=F32)
    s_past = _dot(qbd, kbuf[slot].astype(BF16)) + bias + dq
    c_hi, c_mid, c_lo = _split3(cc)
    s_cur = nt(qbd, kc_ref[0].astype(BF16)) + (nt(neg_sel, c_hi) + nt(neg_sel, c_mid) + nt(neg_sel, c_lo)) + dq
    tq_idx = lax.broadcasted_iota(jnp.int32, (rows, t_cur), 0) % t_cur
    ts_idx = lax.broadcasted_iota(jnp.int32, (rows, t_cur), 1)
    s_cur = jnp.where(ts_idx <= tq_idx, s_cur, MASK_VALUE)

    m = jnp.maximum(jnp.max(s_past, axis=-1, keepdims=True), jnp.max(s_cur, axis=-1, keepdims=True))
    p_past = jnp.exp(s_past - m)
    p_cur = jnp.exp(s_cur - m)
    denom = jnp.sum(p_past, axis=-1, keepdims=True) + jnp.sum(p_cur, axis=-1, keepdims=True)
    o_all = (nt(p_past.astype(BF16), vbuf[slot].astype(BF16))
             + _dot(p_cur.astype(BF16), vc_ref[0].astype(BF16))) / denom
    o_all = jnp.where(rrow == rcol, o_all, 0.0)
    o = o_all[0:t_cur]
    for h in range(1, heads):
        o = o + o_all[h * t_cur:(h + 1) * t_cur]
    o_ref[0] = o.astype(o_ref.dtype)


def _attn_sample(proj, logf, cache_k, cache_v, cache_logf, page_table, layer, cols, heads, dh):
    nb, t_cur, _ = proj.shape
    hw = heads * dh
    depth, n_pool, page = cache_k.shape[:3]
    n_pages = page_table.shape[1]
    ck = jnp.transpose(cache_k, (0, 1, 3, 4, 2)).reshape(depth, n_pool, hw, page)
    cv = jnp.transpose(cache_v, (0, 1, 3, 4, 2)).reshape(depth, n_pool, hw, page)
    clf = jnp.transpose(cache_logf, (0, 1, 3, 2))
    cq, ckc, cvc = cols
    grid_spec = pltpu.PrefetchScalarGridSpec(
        num_scalar_prefetch=1,
        grid=(nb,),
        in_specs=[pl.BlockSpec((1, t_cur, hw), lambda b, pt: (b, 0, cq)),
                  pl.BlockSpec((1, t_cur, hw), lambda b, pt: (b, 0, ckc)),
                  pl.BlockSpec((1, t_cur, hw), lambda b, pt: (b, 0, cvc)),
                  pl.BlockSpec((1, t_cur, LANES), lambda b, pt: (b, 0, 0)),
                  pl.BlockSpec(memory_space=pl.ANY),
                  pl.BlockSpec(memory_space=pl.ANY),
                  pl.BlockSpec(memory_space=pl.ANY)],
        out_specs=pl.BlockSpec((1, t_cur, hw), lambda b, pt: (b, 0, 0)),
        scratch_shapes=[pltpu.VMEM((2, hw, n_pages * page), F32),
                        pltpu.VMEM((2, hw, n_pages * page), F32),
                        pltpu.VMEM((2, n_pages * heads, page), F32),
                        pltpu.SemaphoreType.DMA((3, 2, n_pages))],
    )
    return pl.pallas_call(
        functools.partial(_attn_sample_kernel, layer=layer, n_pages=n_pages, heads=heads, dh=dh),
        grid_spec=grid_spec,
        out_shape=jax.ShapeDtypeStruct((nb, t_cur, hw), BF16),
        compiler_params=_params(("arbitrary",)),
        name="attn_sample",
    )(page_table, proj, proj, proj, logf, ck, cv, clf)


def _hgrn_kernel(q_ref, f_ref, v_ref, og_ref, lb_ref, ng_ref, s0_ref, *rest, chunk, sub):
    y_ref, sout_ref, st_scr = rest[-3:]
    tb, tt, width = q_ref.shape
    heads = s0_ref.shape[2]
    kd = width // heads
    n_chunks = tt // chunk
    n_sub = chunk // sub
    t_idx = pl.program_id(1)
    hb = [slice(h * kd, (h + 1) * kd) for h in range(heads)]
    per_head = lambda fn: jnp.concatenate([fn(h) for h in range(heads)], axis=1)

    @pl.when(t_idx == 0)
    def _():
        for ib in range(tb):
            for h in range(heads):
                st_scr[ib, h] = s0_ref[0, ib, h].T

    lb = lb_ref[...]
    ng = ng_ref[...]
    tri = _tri_lower(chunk)
    sub_col = lax.broadcasted_iota(jnp.int32, (sub, 1), 0)

    def do_chunk(ib, rows):
        g = lb + (1.0 - lb) * _sigmoid(f_ref[ib, rows, :])
        kk = 1.0 - g
        qq = _silu(q_ref[ib, rows, :])
        v = v_ref[ib, rows, :]
        vb = v.astype(BF16)
        bcum = _dot3(tri, jnp.log(g))
        st = [st_scr[ib, h] for h in range(heads)]
        qe = (qq * jnp.exp(bcum)).astype(BF16)
        nt = lambda x, y: lax.dot_general(x, y, NT_DIMS, preferred_element_type=F32)
        o_inter = [nt(qe[:, hb[h]], st[h].astype(BF16)) for h in range(heads)]
        b_last = bcum[chunk - 1:chunk]
        kdec = (kk * jnp.exp(b_last - bcum)).astype(BF16)
        e_last = jnp.exp(b_last)
        st_add = [lax.dot_general(vb[:, hb[h]], kdec[:, hb[h]], TN_DIMS, preferred_element_type=F32)
                  for h in range(heads)]
        cross_sc = {}
        for si in range(1, n_sub):
            lo_r, hi_r = si * sub, (si + 1) * sub
            ref = bcum[lo_r - 1:lo_r]
            a = (qq[lo_r:hi_r] * jnp.exp(bcum[lo_r:hi_r] - ref)).astype(BF16)
            bm = (kk[:lo_r] * jnp.exp(ref - bcum[:lo_r])).astype(BF16)
            cross_sc[si] = [nt(a[:, hb[h]], bm[:, hb[h]]) for h in range(heads)]
        diag = []
        for si in range(n_sub):
            lo_r, hi_r = si * sub, (si + 1) * sub
            qs, ks, vs, bs2 = qq[lo_r:hi_r], kk[lo_r:hi_r], v[lo_r:hi_r], bcum[lo_r:hi_r] * LOG2_E
            acc = None
            for s in range(sub):
                y = qs * ks[s:s + 1] * jnp.exp2(bs2 - bs2[s:s + 1])
                term = per_head(lambda h: jnp.where(sub_col >= s, jnp.sum(y[:, hb[h]], axis=-1, keepdims=True),
                                                    0.0) * vs[s:s + 1, hb[h]])
                acc = term if acc is None else acc + term
            diag.append(acc)
        pieces = []
        for si in range(n_sub):
            lo_r, hi_r = si * sub, (si + 1) * sub
            acc = diag[si] + per_head(lambda h: o_inter[h][lo_r:hi_r])
            if si > 0:
                acc = acc + per_head(lambda h: _dot(cross_sc[si][h].astype(BF16), vb[:lo_r, hb[h]]))
            pieces.append(acc)
        o = jnp.concatenate(pieces, axis=0) if n_sub > 1 else pieces[0]
        for h in range(heads):
            st_scr[ib, h] = e_last[:, hb[h]] * st[h] + st_add[h]
        o = per_head(lambda h: o[:, hb[h]] * lax.rsqrt(jnp.mean(o[:, hb[h]] * o[:, hb[h]], axis=-1, keepdims=True)
                                                       + RMS_EPS))
        y_ref[ib, rows, :] = (o * ng * _silu(og_ref[ib, rows, :])).astype(y_ref.dtype)

    for ib in range(tb):
        if n_chunks == 1:
            do_chunk(ib, slice(0, chunk))
        else:
            def body(c, carry, ib=ib):
                do_chunk(ib, pl.ds(pl.multiple_of(c * chunk, chunk), chunk))
                return carry
            lax.fori_loop(0, n_chunks, body, 0)

    @pl.when(t_idx == pl.num_programs(1) - 1)
    def _():
        for ib in range(tb):
            for h in range(heads):
                sout_ref[0, ib, h] = st_scr[ib, h].T


def _hgrn(proj, s0, l_in, lb, norm_g, cols, layer, depth, stacked):
    nb, g, _ = proj.shape
    hg_heads, kd, vd = s0.shape[2:]
    width = hg_heads * kd
    extra = (stacked,)
    if g >= 64:
        tb, tt = 1, _largest_divisor(g, 512, 64)
        chunk, sub = 64, 8
    else:
        tb, tt = _largest_divisor(nb, 8, 1), g
        chunk, sub = g, g
    col = lambda c0: pl.BlockSpec((tb, tt, width), lambda i, t: (i, t, c0))
    return pl.pallas_call(
        functools.partial(_hgrn_kernel, chunk=chunk, sub=sub),
        grid=(nb // tb, g // tt),
        in_specs=[col(cols[0]), col(cols[1]), col(cols[2]), col(cols[3]),
                  pl.BlockSpec((1, width), lambda i, t: (0, 0)),
                  pl.BlockSpec((1, width), lambda i, t: (0, 0)),
                  pl.BlockSpec((1, tb, hg_heads, kd, vd), lambda i, t: (l_in, i, 0, 0, 0))]
                 + [pl.BlockSpec(memory_space=pl.ANY)] * len(extra),
        out_specs=[pl.BlockSpec((tb, tt, width), lambda i, t: (i, t, 0)),
                   pl.BlockSpec((1, tb, hg_heads, kd, vd), lambda i, t: (layer, i, 0, 0, 0))],
        out_shape=[jax.ShapeDtypeStruct((nb, g, width), BF16),
                   jax.ShapeDtypeStruct((depth, nb, hg_heads, kd, vd), F32)],
        input_output_aliases={7 + i: 1 for i in range(len(extra))},
        scratch_shapes=[pltpu.VMEM((tb, hg_heads, vd, kd), F32)],
        compiler_params=_params(("parallel", "arbitrary")),
        name="hgrn",
    )(proj, proj, proj, proj, lb.reshape(1, -1), norm_g.reshape(1, -1), s0, *extra)


def _gelu(x):
    return 0.5 * x * (1.0 + lax.erf(x * (2.0 ** -0.5)))


def _mixffn_kernel(ya_ref, yb_ref, ga_ref, gb_ref, x_ref, gm_ref, shf_ref, scf_ref, gf_ref,
                   wa_ref, wb_ref, wo_ref, l1g_ref, l1b_ref, wu_ref, cw_ref, cb_ref, wd_ref, l2g_ref, l2b_ref,
                   prev_ref, x2_ref, cnew_ref, act_scr, carry_scr, *, alpha, tf):
    tb, tg, d = x_ref.shape
    tm = tb * tg
    dff = wd_ref.shape[0]
    first = pl.program_id(1) == 0

    y_a = _dot(ya_ref[...].reshape(tm, -1), wa_ref[...])
    y_b = _dot(yb_ref[...].reshape(tm, -1), wb_ref[...])
    merged = _sigmoid(ga_ref[...].reshape(tm, d)) * y_a + _sigmoid(gb_ref[...].reshape(tm, d)) * y_b
    z = _dot(merged.astype(BF16), wo_ref[...]).reshape(tb, tg, d)
    x1 = _layer_norm(alpha * x_ref[...] + (1.0 + gm_ref[...]) * z, l1g_ref[...], l1b_ref[...])
    h = (x1 * (1.0 + scf_ref[...]) + shf_ref[...]).reshape(tm, d).astype(BF16)

    @pl.when(first)
    def _():
        carry_scr[...] = jnp.zeros_like(carry_scr)

    tt = lax.broadcasted_iota(jnp.int32, (tb, tg, tf), 1)
    for c in range(dff // tf):
        cols = slice(c * tf, (c + 1) * tf)
        a = _dot(h, wu_ref[:, cols])
        gate = _dot(h, wu_ref[:, dff + c * tf:dff + (c + 1) * tf])
        a3 = a.reshape(tb, tg, tf)
        tail = carry_scr[:, :, cols]
        p0 = jnp.where(first, prev_ref[:, 0:1, cols], tail[:, SUBLANES - 2:SUBLANES - 1, :])
        p1 = jnp.where(first, prev_ref[:, 1:2, cols], tail[:, SUBLANES - 1:SUBLANES, :])
        am1 = jnp.where(tt == 0, p1, pltpu.roll(a, 1, 0).reshape(tb, tg, tf))
        am2 = jnp.where(tt == 0, p0, jnp.where(tt == 1, p1, pltpu.roll(a, 2, 0).reshape(tb, tg, tf)))
        conv = cb_ref[:, cols] + cw_ref[0:1, cols] * am2 + cw_ref[1:2, cols] * am1 + cw_ref[2:3, cols] * a3
        act_scr[:, cols] = (_gelu(conv) * gate.reshape(tb, tg, tf)).astype(BF16).reshape(tm, tf)
        carry_scr[:, :, cols] = a3[:, tg - SUBLANES:, :]
        cnew_ref[:, 0, :, cols] = a3[:, tg - 2:, :]

    y = _dot(act_scr[...], wd_ref[...]).reshape(tb, tg, d)
    x2_ref[...] = _layer_norm(alpha * x1 + (1.0 + gf_ref[...]) * y, l2g_ref[...], l2b_ref[...])


def _mixffn(attn, yb, proj, x3, ada3, prev, w_br_a, w_br_b, w_o, ln1_g, ln1_b, w_up, conv_w, conv_b, w_down,
            ln2_g, ln2_b, alpha, gate_cols, rows=512, tf=256):
    nb, g, d = x3.shape
    dff = w_down.shape[0]
    tb, tg = _row_tiles(nb, g, rows)
    ca, cb = gate_cols
    tok = lambda w, c: pl.BlockSpec((tb, tg, w), lambda i, t: (i, t, c))
    mod = lambda c: pl.BlockSpec((tb, 1, d), lambda i, t: (i, 0, c))
    resident = lambda shape: pl.BlockSpec(shape, lambda i, t: (0,) * len(shape), pipeline_mode=pl.Buffered(1))
    vec = lambda a: a.reshape(1, -1)
    return pl.pallas_call(
        functools.partial(_mixffn_kernel, alpha=alpha, tf=tf),
        grid=(nb // tb, g // tg),
        in_specs=[tok(attn.shape[-1], 0), tok(yb.shape[-1], 0), tok(d, ca), tok(d, cb), tok(d, 0),
                  mod(2), mod(3), mod(4), mod(5),
                  resident(w_br_a.shape), resident(w_br_b.shape), resident(w_o.shape),
                  resident((1, d)), resident((1, d)),
                  resident(w_up.shape), resident(conv_w.shape), resident((1, dff)), resident(w_down.shape),
                  resident((1, d)), resident((1, d)),
                  pl.BlockSpec((tb, 2, dff), lambda i, t: (i, 0, 0))],
        out_specs=[tok(d, 0),
                   pl.BlockSpec((tb, 1, 2, dff), lambda i, t: (i, t, 0, 0))],
        out_shape=[jax.ShapeDtypeStruct((nb, g, d), F32), jax.ShapeDtypeStruct((nb, g // tg, 2, dff), F32)],
        scratch_shapes=[pltpu.VMEM((tb * tg, dff), BF16), pltpu.VMEM((tb, SUBLANES, dff), F32)],
        compiler_params=_params(("parallel", "arbitrary")),
        name="mixffn",
    )(attn, yb, proj, proj, x3, ada3, ada3, ada3, ada3, w_br_a, w_br_b, w_o, vec(ln1_g), vec(ln1_b),
      w_up, conv_w, vec(conv_b), w_down, vec(ln2_g), vec(ln2_b), prev)


def kernel(x_prompt, x_sample, cache_k, cache_v, cache_logf, state_hgrn, state_conv, page_table, c_prompt, c_sample, ln0_g, ln0_b, w_in, fox_f_bias, hg_lb, hg_norm_g, w_br_a, w_br_b, w_o, ln1_g, ln1_b, w_up, conv_w, conv_b, w_down, ln2_g, ln2_b, w_ada, b_ada):
    bp, tp, d = x_prompt.shape
    bs, ts, _ = x_sample.shape
    depth = w_in.shape[0]
    heads, dh = cache_k.shape[3], cache_k.shape[4]
    hw = heads * dh
    hg_heads, kd, vd = state_hgrn.shape[2:]
    hgw = hg_heads * kd
    dff = w_down.shape[1]
    assert conv_w.shape[1] == 3 and state_conv.shape[2] == 2
    assert kd == LANES and vd == LANES and 2 * dh == LANES and hw == hgw == hg_heads * vd and d == 2 * hw
    alpha = (2 * depth) ** 0.25

    lb_soft = jax.nn.softmax(hg_lb.astype(F32), axis=0)
    lb_all = jnp.cumsum(lb_soft, axis=0) - lb_soft[0]

    o_f = 3 * hw
    o_b = o_f + heads
    o_g = o_b + 4 * hgw
    w_main = jnp.concatenate([w_in[:, :, o_g:], w_in[:, :, o_b:o_g], w_in[:, :, :o_f]], axis=2).astype(BF16)
    w_f = jnp.pad(w_in[:, :, o_f:o_b], ((0, 0), (0, 0), (0, LANES - heads))).astype(BF16)
    f_bias = jnp.pad(fox_f_bias, ((0, 0), (0, LANES - heads))).reshape(depth, 1, LANES)
    w_rest = 2 * d + 4 * hgw
    gate_cols = (0, 1)
    hg_cols = tuple(2 * d // hgw + i for i in range(4))
    qkv_cols = tuple(w_rest // hw + i for i in range(3))
    k_off, v_off = w_rest + hw, w_rest + 2 * hw
    w_br_a16, w_br_b16, w_o16 = w_br_a.astype(BF16), w_br_b.astype(BF16), w_o.astype(BF16)
    w_up16, w_down16 = w_up.astype(BF16), w_down.astype(BF16)
    consts = _aug_constants(heads, dh)

    ada = _ada(jnp.concatenate([c_prompt, c_sample], axis=0), w_ada, b_ada)
    xp = _ln0(x_prompt.reshape(bp * tp, d), ln0_g, ln0_b).reshape(bp, tp, d)
    xs = _ln0(x_sample.reshape(bs * ts, d), ln0_g, ln0_b).reshape(bs, ts, d)
    s0_p = jnp.zeros((1, bp) + state_hgrn.shape[2:], F32)
    conv0_p = jnp.zeros((bp, 2, dff), F32)

    kv_t = [jnp.zeros((depth, bp, hw, tp), F32), jnp.zeros((depth, bp, hw, tp), F32)]
    hg_p = jnp.zeros((depth, bp) + state_hgrn.shape[2:], F32)
    hg_s = jnp.zeros((depth, bs) + state_hgrn.shape[2:], F32)
    outs_p, outs_s = [], []
    for l in range(depth):
        ada_p = ada[l, :bp].reshape(bp, 1, 6 * d)
        ada_s = ada[l, bp:].reshape(bs, 1, 6 * d)

        proj, logf = _inproj(xp, ada_p, w_main[l], w_f[l], f_bias[l], width=w_rest)
        qa, ka, va_t, *kv_t = _attn_prep(xp, ada_p, w_main[l], logf, consts, qkv_cols, heads, dh, l, depth, kv_t)
        attn = _attn_prompt(qa, ka, va_t, heads, dh)
        yb, hg_p = _hgrn(proj, s0_p, 0, lb_all[l], hg_norm_g[l], hg_cols, l, depth, hg_p)
        xp, conv_new = _mixffn(attn, yb, proj, xp, ada_p, conv0_p, w_br_a16[l], w_br_b16[l], w_o16[l],
                               ln1_g[l], ln1_b[l], w_up16[l], conv_w[l], conv_b[l], w_down16[l],
                               ln2_g[l], ln2_b[l], alpha, gate_cols)
        outs_p.append((logf[:, :, :heads], conv_new[:, -1]))

        proj, logf = _inproj(xs, ada_s, w_main[l], w_f[l], f_bias[l], width=w_main.shape[2])
        attn = _attn_sample(proj, logf, cache_k, cache_v, cache_logf, page_table, l, qkv_cols, heads, dh)
        yb, hg_s = _hgrn(proj, state_hgrn, l, lb_all[l], hg_norm_g[l], hg_cols, l, depth, hg_s)
        xs, conv_new = _mixffn(attn, yb, proj, xs, ada_s, state_conv[l], w_br_a16[l], w_br_b16[l], w_o16[l],
                               ln1_g[l], ln1_b[l], w_up16[l], conv_w[l], conv_b[l], w_down16[l],
                               ln2_g[l], ln2_b[l], alpha, gate_cols)
        outs_s.append((proj[:, :, k_off:k_off + hw].reshape(bs, ts, heads, dh),
                       proj[:, :, v_off:v_off + hw].reshape(bs, ts, heads, dh),
                       logf[:, :, :heads], conv_new[:, -1]))

    stack = lambda outs, i: jnp.stack([o[i] for o in outs])
    k_p, v_p = (jnp.transpose(a.reshape(depth, bp, heads, dh, tp), (0, 1, 4, 2, 3)) for a in kv_t)
    return (xp, xs, k_p, v_p, stack(outs_p, 0), hg_p, stack(outs_p, 1),
            stack(outs_s, 0), stack(outs_s, 1), stack(outs_s, 2), hg_s, stack(outs_s, 3))
```

```python
import functools

import numpy as np
import jax
import jax.numpy as jnp
from jax import lax
from jax.experimental import pallas as pl
from jax.experimental.pallas import tpu as pltpu

F32 = jnp.float32
BF16 = jnp.bfloat16

LN_EPS = 1e-5
RMS_EPS = 1e-6
MASK_VALUE = -1e30
LOG2_E = 1.4426950408889634
LANES = 128
SUBLANES = 8
VMEM_LIMIT = 56 * 1024 * 1024

NT_DIMS = (((1,), (1,)), ((), ()))
TN_DIMS = (((0,), (0,)), ((), ()))


def _params(sem):
    return pltpu.CompilerParams(dimension_semantics=sem, vmem_limit_bytes=VMEM_LIMIT)


def _largest_divisor(n, target, mult):
    best = None
    for d in range(1, n + 1):
        if n % d == 0 and d <= target and d % mult == 0:
            best = d
    return n if best is None else best


def _row_tiles(nb, g, target):
    if g >= target:
        return 1, _largest_divisor(g, target, SUBLANES)
    return _largest_divisor(nb, max(1, target // g), 1), g


def _split3(x):
    hi = x.astype(BF16)
    r1 = x - hi.astype(F32)
    mid = r1.astype(BF16)
    lo = (r1 - mid.astype(F32)).astype(BF16)
    return hi, mid, lo


def _dot(a, b):
    return jnp.dot(a, b, preferred_element_type=F32)


def _dot3(a_bf16, x_f32):
    hi, mid, lo = _split3(x_f32)
    return _dot(a_bf16, hi) + _dot(a_bf16, mid) + _dot(a_bf16, lo)


def _layer_norm(x, g, b):
    mu = jnp.mean(x, axis=-1, keepdims=True)
    xc = x - mu
    var = jnp.mean(xc * xc, axis=-1, keepdims=True)
    return xc * lax.rsqrt(var + LN_EPS) * g + b


def _sigmoid(x):
    return 1.0 / (1.0 + jnp.exp(-x))


def _silu(x):
    return x * _sigmoid(x)


def _log_sigmoid(x):
    return jnp.minimum(x, 0.0) - jnp.log1p(jnp.exp(-jnp.abs(x)))


def _tri_lower(n):
    r = lax.broadcasted_iota(jnp.int32, (n, n), 0)
    c = lax.broadcasted_iota(jnp.int32, (n, n), 1)
    return jnp.where(r >= c, 1.0, 0.0).astype(BF16)


def _ln0_kernel(x_ref, g_ref, b_ref, o_ref):
    o_ref[...] = _layer_norm(x_ref[...], g_ref[...], b_ref[...])


def _ln0(x2, g, b):
    n, d = x2.shape
    tm = _largest_divisor(n, 1024, SUBLANES)
    return pl.pallas_call(
        _ln0_kernel,
        grid=(n // tm,),
        in_specs=[pl.BlockSpec((tm, d), lambda i: (i, 0)),
                  pl.BlockSpec((1, d), lambda i: (0, 0)),
                  pl.BlockSpec((1, d), lambda i: (0, 0))],
        out_specs=pl.BlockSpec((tm, d), lambda i: (i, 0)),
        out_shape=jax.ShapeDtypeStruct((n, d), F32),
        compiler_params=_params(("parallel",)),
        name="ln0",
    )(x2, g.reshape(1, d), b.reshape(1, d))


def _ada_kernel(c_ref, w_ref, b_ref, o_ref):
    s = _silu(c_ref[...]).astype(BF16)
    o_ref[0] = _dot(s, w_ref[0].astype(BF16)) + b_ref[0]


def _ada(c_all, w_ada, b_ada):
    depth, d, n6 = w_ada.shape
    nc = c_all.shape[0]
    tn = _largest_divisor(n6, 1536, LANES)
    return pl.pallas_call(
        _ada_kernel,
        grid=(depth, n6 // tn),
        in_specs=[pl.BlockSpec((nc, d), lambda l, j: (0, 0)),
                  pl.BlockSpec((1, d, tn), lambda l, j: (l, 0, j)),
                  pl.BlockSpec((1, 1, tn), lambda l, j: (l, 0, j))],
        out_specs=pl.BlockSpec((1, nc, tn), lambda l, j: (l, 0, j)),
        out_shape=jax.ShapeDtypeStruct((depth, nc, n6), F32),
        compiler_params=_params(("parallel", "parallel")),
        name="ada",
    )(c_all, w_ada, b_ada.reshape(depth, 1, n6))


def _inproj_kernel(x_ref, sh_ref, sc_ref, w_ref, wf_ref, fb_ref, proj_ref, logf_ref, *, tn):
    tb, tg, d = x_ref.shape
    h = (x_ref[...] * (1.0 + sc_ref[...]) + sh_ref[...]).reshape(tb * tg, d).astype(BF16)
    f = _dot(h, wf_ref[...]) + fb_ref[...]
    logf_ref[...] = _log_sigmoid(f).reshape(tb, tg, LANES)
    for c in range(w_ref.shape[1] // tn):
        cols = slice(c * tn, (c + 1) * tn)
        proj_ref[:, :, cols] = _dot(h, w_ref[:, cols]).reshape(tb, tg, tn)


def _inproj(x3, ada3, w_main, w_f, f_bias, width, tn=512, rows=512):
    nb, g, d = x3.shape
    tb, tg = _row_tiles(nb, g, rows)
    resident = lambda shape: pl.BlockSpec(shape, lambda i, t: (0,) * len(shape), pipeline_mode=pl.Buffered(1))
    return pl.pallas_call(
        functools.partial(_inproj_kernel, tn=tn),
        grid=(nb // tb, g // tg),
        in_specs=[pl.BlockSpec((tb, tg, d), lambda i, t: (i, t, 0)),
                  pl.BlockSpec((tb, 1, d), lambda i, t: (i, 0, 0)),
                  pl.BlockSpec((tb, 1, d), lambda i, t: (i, 0, 1)),
                  resident((d, width)),
                  resident((d, LANES)),
                  resident((1, LANES))],
        out_specs=[pl.BlockSpec((tb, tg, width), lambda i, t: (i, t, 0)),
                   pl.BlockSpec((tb, tg, LANES), lambda i, t: (i, t, 0))],
        out_shape=[jax.ShapeDtypeStruct((nb, g, width), F32),
                   jax.ShapeDtypeStruct((nb, g, LANES), F32)],
        compiler_params=_params(("parallel", "parallel")),
        name="inproj",
    )(x3, ada3, ada3, w_main, w_f, f_bias)


def _aug_constants(heads, dh):
    width = heads * LANES
    pq = np.zeros((3, LANES, width), np.float32)
    pk = np.zeros((3, LANES, width), np.float32)
    ones_q = np.zeros((1, width), np.float32)
    ones_k = np.zeros((1, width), np.float32)
    own = np.zeros((1, width), np.float32)
    for h in range(heads):
        o_h = (h % 2) * dh
        a_h = h * LANES + (dh - o_h)
        own[0, h * LANES + o_h:h * LANES + o_h + dh] = 1.0
        for i in range(3):
            pq[i, h, a_h + i] = 1.0
            pk[i, h, a_h + 3 + i] = 1.0
            ones_q[0, a_h + 3 + i] = 1.0
            ones_k[0, a_h + i] = 1.0
    return (jnp.asarray(pq, BF16), jnp.asarray(pk, BF16), jnp.asarray(ones_q), jnp.asarray(ones_k),
            jnp.asarray(own))


def _prep_kernel(x_ref, sh_ref, sc_ref, wq_ref, wk_ref, wv_ref, lf_ref, pq_ref, pk_ref, oq_ref, ok_ref, own_ref,
                 *rest, scale):
    qa_ref, ka_ref, va_ref, kt_ref, vt_ref, carry = rest[-6:]
    tc = x_ref.shape[1]
    heads = qa_ref.shape[2] // LANES

    @pl.when(pl.program_id(1) == 0)
    def _():
        carry[...] = jnp.zeros_like(carry)

    h = (x_ref[0] * (1.0 + sc_ref[0]) + sh_ref[0]).astype(BF16)
    q = _dot(h, wq_ref[...])
    k = _dot(h, wk_ref[...])
    v = _dot(h, wv_ref[...])

    d = _dot3(_tri_lower(tc), lf_ref[0]) + carry[...]
    carry[...] = d[tc - 1:tc, :]
    hi, mid, lo = _split3(d * LOG2_E)
    aux_q = _dot(hi, pq_ref[0]) + _dot(mid, pq_ref[1]) + _dot(lo, pq_ref[2]) + oq_ref[...]
    aux_k = ok_ref[...] - (_dot(hi, pk_ref[0]) + _dot(mid, pk_ref[1]) + _dot(lo, pk_ref[2]))
    own = own_ref[...] > 0.5

    def widen(x):
        return jnp.concatenate([x[:, (h // 2) * LANES:(h // 2 + 1) * LANES] for h in range(heads)], axis=1)

    qa_ref[0] = jnp.where(own, widen(q) * (scale * LOG2_E), aux_q).astype(BF16)
    ka_ref[0] = jnp.where(own, widen(k), aux_k).astype(BF16)
    dh = LANES // 2
    row = lax.broadcasted_iota(jnp.int32, (LANES, tc), 0)
    for hp in range(heads // 2):
        pair = slice(hp * LANES, (hp + 1) * LANES)
        kt_ref[0, 0, pair, :] = k[:, pair].T
        vt = v[:, pair].T
        vt_ref[0, 0, pair, :] = vt
        va_ref[0, (2 * hp) * LANES:(2 * hp + 1) * LANES, :] = jnp.where(
            row < dh, vt, jnp.where(row == dh, 1.0, 0.0)).astype(BF16)
        va_ref[0, (2 * hp + 1) * LANES:(2 * hp + 2) * LANES, :] = jnp.where(
            row >= dh, vt, jnp.where(row == 0, 1.0, 0.0)).astype(BF16)


def _attn_prep(x3, ada3, w_main, logf, consts, cols, heads, dh, layer, depth, stacked):
    b, t, d = x3.shape
    hw = heads * dh
    width = heads * LANES
    tc = _largest_divisor(t, 512, SUBLANES)
    pq, pk, oq, ok, own = consts
    cq, ck, cv = cols
    full = lambda shape: pl.BlockSpec(shape, lambda i, j: (0,) * len(shape))
    out = jax.ShapeDtypeStruct((b, t, width), BF16)
    cache = jax.ShapeDtypeStruct((depth, b, hw, t), F32)
    cache_spec = pl.BlockSpec((1, 1, hw, tc), lambda i, j: (layer, i, 0, j))
    extra = tuple(stacked)
    n_in = 12
    return pl.pallas_call(
        functools.partial(_prep_kernel, scale=dh ** -0.5),
        grid=(b, t // tc),
        in_specs=[pl.BlockSpec((1, tc, d), lambda i, j: (i, j, 0)),
                  pl.BlockSpec((1, 1, d), lambda i, j: (i, 0, 0)),
                  pl.BlockSpec((1, 1, d), lambda i, j: (i, 0, 1)),
                  pl.BlockSpec((d, hw), lambda i, j: (0, cq)),
                  pl.BlockSpec((d, hw), lambda i, j: (0, ck)),
                  pl.BlockSpec((d, hw), lambda i, j: (0, cv)),
                  pl.BlockSpec((1, tc, LANES), lambda i, j: (i, j, 0)),
                  full(pq.shape), full(pk.shape), full(oq.shape), full(ok.shape), full(own.shape)]
                 + [pl.BlockSpec(memory_space=pl.ANY)] * len(extra),
        out_specs=[pl.BlockSpec((1, tc, width), lambda i, j: (i, j, 0)),
                   pl.BlockSpec((1, tc, width), lambda i, j: (i, j, 0)),
                   pl.BlockSpec((1, width, tc), lambda i, j: (i, 0, j)),
                   cache_spec, cache_spec],
        out_shape=[out, out, jax.ShapeDtypeStruct((b, width, t), BF16), cache, cache],
        input_output_aliases={n_in + i: 3 + i for i in range(len(extra))},
        scratch_shapes=[pltpu.VMEM((1, LANES), F32)],
        compiler_params=_params(("parallel", "arbitrary")),
        name="attn_prep",
    )(x3, ada3, ada3, w_main, w_main, w_main, logf, pq, pk, oq, ok, own, *extra)


def _attn_kernel(qi_ref, kj_ref, q_ref, k_ref, v_ref, o_ref, m_scr, acc_scr, *, dh):
    tq = q_ref.shape[1]
    tk = k_ref.shape[1]
    heads = q_ref.shape[2] // LANES
    step = pl.program_id(1)
    i = qi_ref[step]
    j = kj_ref[step]

    @pl.when(j == 0)
    def _():
        m_scr[...] = jnp.full_like(m_scr, MASK_VALUE)
        acc_scr[...] = jnp.zeros_like(acc_scr)

    def update(masked):
        if masked:
            key = lax.broadcasted_iota(jnp.int32, (tk, tq), 0)
            qry = lax.broadcasted_iota(jnp.int32, (tk, tq), 1)
            keep = key <= qry

        def scores(h):
            blk = slice(h * LANES, (h + 1) * LANES)
            s = lax.dot_general(k_ref[0, :, blk], q_ref[0, :, blk], NT_DIMS, preferred_element_type=F32)
            if masked:
                s = jnp.where(keep, s, MASK_VALUE)
            m_prev = m_scr[h]
            return s, m_prev, jnp.maximum(m_prev, jnp.max(s, axis=0, keepdims=True))

        def accumulate(h, s, m_prev, m_new):
            p = jnp.exp2(s - m_new).astype(BF16)
            acc_scr[h] = jnp.exp2(m_prev - m_new) * acc_scr[h] + _dot(v_ref[0, h * LANES:(h + 1) * LANES, :], p)
            m_scr[h] = m_new

        ahead = 2
        pending = [scores(h) for h in range(ahead)]
        for h in range(heads):
            if h + ahead < heads:
                pending.append(scores(h + ahead))
            accumulate(h, *pending.pop(0))

    @pl.when(j < i)
    def _():
        update(False)

    @pl.when(j == i)
    def _():
        update(True)
        row = lax.broadcasted_iota(jnp.int32, (LANES, tq), 0)
        for hp in range(heads // 2):
            a0 = acc_scr[2 * hp]
            a1 = acc_scr[2 * hp + 1]
            o_t = jnp.where(row < dh, a0 / a0[dh:dh + 1, :], a1 / a1[0:1, :])
            o_ref[0, :, hp * LANES:(hp + 1) * LANES] = o_t.T.astype(o_ref.dtype)


def _attn_prompt(qa, ka, va, heads, dh):
    b, t, width = qa.shape
    tq = _largest_divisor(t, 512, SUBLANES)
    nq = t // tq
    qi = np.array([i for i in range(nq) for j in range(i + 1)], np.int32)
    kj = np.array([j for i in range(nq) for j in range(i + 1)], np.int32)
    grid_spec = pltpu.PrefetchScalarGridSpec(
        num_scalar_prefetch=2,
        grid=(b, len(qi)),
        in_specs=[pl.BlockSpec((1, tq, width), lambda bb, s, qi_r, kj_r: (bb, qi_r[s], 0)),
                  pl.BlockSpec((1, tq, width), lambda bb, s, qi_r, kj_r: (bb, kj_r[s], 0)),
                  pl.BlockSpec((1, width, tq), lambda bb, s, qi_r, kj_r: (bb, 0, kj_r[s]))],
        out_specs=pl.BlockSpec((1, tq, heads * dh), lambda bb, s, qi_r, kj_r: (bb, qi_r[s], 0)),
        scratch_shapes=[pltpu.VMEM((heads, 1, tq), F32), pltpu.VMEM((heads, LANES, tq), F32)],
    )
    return pl.pallas_call(
        functools.partial(_attn_kernel, dh=dh),
        grid_spec=grid_spec,
        out_shape=jax.ShapeDtypeStruct((b, t, heads * dh), BF16),
        compiler_params=_params(("parallel", "arbitrary")),
        name="attn_prompt",
    )(jnp.asarray(qi), jnp.asarray(kj), qa, ka, va)


def _attn_sample_kernel(pt_ref, q_ref, kc_ref, vc_ref, lfc_ref, ck_hbm, cv_hbm, clf_hbm, o_ref,
                        kbuf, vbuf, lbuf, sems, *, layer, n_pages, heads, dh):
    b = pl.program_id(0)
    nb = pl.num_programs(0)
    page = lbuf.shape[2]
    hw = heads * dh
    t_cur = q_ref.shape[1]
    rows = heads * t_cur
    npr = n_pages * heads

    def copies(seq, slot):
        out = []
        for p in range(n_pages):
            pid = pt_ref[seq, p]
            lanes = pl.ds(p * page, page)
            out.append(pltpu.make_async_copy(ck_hbm.at[layer, pid], kbuf.at[slot, :, lanes], sems.at[0, slot, p]))
            out.append(pltpu.make_async_copy(cv_hbm.at[layer, pid], vbuf.at[slot, :, lanes], sems.at[1, slot, p]))
            out.append(pltpu.make_async_copy(clf_hbm.at[layer, pid], lbuf.at[slot, pl.ds(p * heads, heads), :],
                                             sems.at[2, slot, p]))
        return out

    @pl.when(b == 0)
    def _():
        for c in copies(0, 0):
            c.start()

    slot = b % 2

    @pl.when(b + 1 < nb)
    def _():
        for c in copies(b + 1, 1 - slot):
            c.start()

    for c in copies(b, slot):
        c.wait()

    ur = lax.broadcasted_iota(jnp.int32, (page, page), 0)
    uc = lax.broadcasted_iota(jnp.int32, (page, page), 1)
    triu = jnp.where(ur <= uc, 1.0, 0.0).astype(BF16)
    l_hi, l_mid, l_lo = _split3(lbuf[slot])
    cum = _dot(l_hi, triu) + _dot(l_mid, triu) + _dot(l_lo, triu)
    mr = lax.broadcasted_iota(jnp.int32, (npr, npr), 0)
    mc = lax.broadcasted_iota(jnp.int32, (npr, npr), 1)
    later = jnp.where((mr % heads == mc % heads) & (mc // heads >= mr // heads), 1.0, 0.0).astype(BF16)
    e = cum - _dot3(later, jnp.broadcast_to(cum[:, page - 1:page], (npr, page)))
    rr = lax.broadcasted_iota(jnp.int32, (n_pages * rows, npr), 0)
    rc = lax.broadcasted_iota(jnp.int32, (n_pages * rows, npr), 1)
    spread = jnp.where((rr // rows == rc // heads) & ((rr % rows) // t_cur == rc % heads), -1.0, 0.0).astype(BF16)
    bias_rows = _dot3(spread, e)
    bias = jnp.concatenate([bias_rows[p * rows:(p + 1) * rows] for p in range(n_pages)], axis=1)

    cc = _dot3(_tri_lower(t_cur), lfc_ref[0])
    rrow = lax.broadcasted_iota(jnp.int32, (rows, hw), 0) // t_cur
    rcol = lax.broadcasted_iota(jnp.int32, (rows, hw), 1) // dh
    q_rep = jnp.concatenate([q_ref[0]] * heads, axis=0)
    qbd = jnp.where(rrow == rcol, q_rep * (dh ** -0.5), 0.0).astype(BF16)
    arow = lax.broadcasted_iota(jnp.int32, (rows, LANES), 0) // t_cur
    acol = lax.broadcasted_iota(jnp.int32, (rows, LANES), 1)
    sel = arow == acol
    neg_sel = jnp.where(sel, -1.0, 0.0).astype(BF16)
    cc_rep = jnp.concatenate([cc] * heads, axis=0)
    dq = jnp.sum(jnp.where(sel, cc_rep, 0.0), axis=-1, keepdims=True)

    nt = lambda a, bm: lax.dot_general(a, bm, NT_DIMS, preferred_element_type=F32)
    s_past = _dot(qbd, kbuf[slot].astype(BF16)) + bias + dq
    c_hi, c_mid, c_lo = _split3(cc)
    s_cur = nt(qbd, kc_ref[0].astype(BF16)) + (nt(neg_sel, c_hi) + nt(neg_sel, c_mid) + nt(neg_sel, c_lo)) + dq
    tq_idx = lax.broadcasted_iota(jnp.int32, (rows, t_cur), 0) % t_cur
    ts_idx = lax.broadcasted_iota(jnp.int32, (rows, t_cur), 1)
    s_cur = jnp.where(ts_idx <= tq_idx, s_cur, MASK_VALUE)

    m = jnp.maximum(jnp.max(s_past, axis=-1, keepdims=True), jnp.max(s_cur, axis=-1, keepdims=True))
    p_past = jnp.exp(s_past - m)
    p_cur = jnp.exp(s_cur - m)
    denom = jnp.sum(p_past, axis=-1, keepdims=True) + jnp.sum(p_cur, axis=-1, keepdims=True)
    o_all = (nt(p_past.astype(BF16), vbuf[slot].astype(BF16))
             + _dot(p_cur.astype(BF16), vc_ref[0].astype(BF16))) / denom
    o_all = jnp.where(rrow == rcol, o_all, 0.0)
    o = o_all[0:t_cur]
    for h in range(1, heads):
        o = o + o_all[h * t_cur:(h + 1) * t_cur]
    o_ref[0] = o.astype(o_ref.dtype)


def _attn_sample(proj, logf, cache_k, cache_v, cache_logf, page_table, layer, cols, heads, dh):
    nb, t_cur, _ = proj.shape
    hw = heads * dh
    depth, n_pool, page = cache_k.shape[:3]
    n_pages = page_table.shape[1]
    ck = jnp.transpose(cache_k, (0, 1, 3, 4, 2)).reshape(depth, n_pool, hw, page)
    cv = jnp.transpose(cache_v, (0, 1, 3, 4, 2)).reshape(depth, n_pool, hw, page)
    clf = jnp.transpose(cache_logf, (0, 1, 3, 2))
    cq, ckc, cvc = cols
    grid_spec = pltpu.PrefetchScalarGridSpec(
        num_scalar_prefetch=1,
        grid=(nb,),
        in_specs=[pl.BlockSpec((1, t_cur, hw), lambda b, pt: (b, 0, cq)),
                  pl.BlockSpec((1, t_cur, hw), lambda b, pt: (b, 0, ckc)),
                  pl.BlockSpec((1, t_cur, hw), lambda b, pt: (b, 0, cvc)),
                  pl.BlockSpec((1, t_cur, LANES), lambda b, pt: (b, 0, 0)),
                  pl.BlockSpec(memory_space=pl.ANY),
                  pl.BlockSpec(memory_space=pl.ANY),
                  pl.BlockSpec(memory_space=pl.ANY)],
        out_specs=pl.BlockSpec((1, t_cur, hw), lambda b, pt: (b, 0, 0)),
        scratch_shapes=[pltpu.VMEM((2, hw, n_pages * page), F32),
                        pltpu.VMEM((2, hw, n_pages * page), F32),
                        pltpu.VMEM((2, n_pages * heads, page), F32),
                        pltpu.SemaphoreType.DMA((3, 2, n_pages))],
    )
    return pl.pallas_call(
        functools.partial(_attn_sample_kernel, layer=layer, n_pages=n_pages, heads=heads, dh=dh),
        grid_spec=grid_spec,
        out_shape=jax.ShapeDtypeStruct((nb, t_cur, hw), BF16),
        compiler_params=_params(("arbitrary",)),
        name="attn_sample",
    )(page_table, proj, proj, proj, logf, ck, cv, clf)


def _hgrn_kernel(q_ref, f_ref, v_ref, og_ref, lb_ref, ng_ref, s0_ref, *rest, chunk, sub):
    y_ref, sout_ref, st_scr = rest[-3:]
    tb, tt, width = q_ref.shape
    heads = s0_ref.shape[2]
    kd = width // heads
    n_chunks = tt // chunk
    n_sub = chunk // sub
    t_idx = pl.program_id(1)
    hb = [slice(h * kd, (h + 1) * kd) for h in range(heads)]
    per_head = lambda fn: jnp.concatenate([fn(h) for h in range(heads)], axis=1)

    @pl.when(t_idx == 0)
    def _():
        for ib in range(tb):
            for h in range(heads):
                st_scr[ib, h] = s0_ref[0, ib, h].T

    lb = lb_ref[...]
    ng = ng_ref[...]
    tri = _tri_lower(chunk)
    sub_col = lax.broadcasted_iota(jnp.int32, (sub, 1), 0)
    half = sub // 2
    upper = lax.broadcasted_iota(jnp.int32, (chunk, 1), 0) % sub >= half
    same_sub = (lax.broadcasted_iota(jnp.int32, (chunk, chunk), 0) // sub
                == lax.broadcasted_iota(jnp.int32, (chunk, chunk), 1) // sub)

    nt = lambda x, y: lax.dot_general(x, y, NT_DIMS, preferred_element_type=F32)

    def prep(ib, rows):
        c = {"ib": ib, "rows": rows}
        g = lb + (1.0 - lb) * _sigmoid(f_ref[ib, rows, :])
        kk = 1.0 - g
        qq = _silu(q_ref[ib, rows, :])
        v = v_ref[ib, rows, :]
        bcum = _dot3(tri, jnp.log(g))
        b_last = bcum[chunk - 1:chunk]
        b2 = bcum * LOG2_E
        c.update(kk=kk, qq=qq, v=v, vb=v.astype(BF16), b2=b2, e_last=jnp.exp(b_last),
                 qe=(qq * jnp.exp(bcum)).astype(BF16), kdec=(kk * jnp.exp(b_last - bcum)).astype(BF16))
        cross = []
        for si in range(1, n_sub):
            lo_r, hi_r = si * sub, (si + 1) * sub
            ref = bcum[lo_r - 1:lo_r]
            cross.append(((qq[lo_r:hi_r] * jnp.exp(bcum[lo_r:hi_r] - ref)).astype(BF16),
                          (kk[:lo_r] * jnp.exp(ref - bcum[:lo_r])).astype(BF16)))
        mid = jnp.concatenate([jnp.broadcast_to(b2[si * sub + half - 1:si * sub + half], (sub, width))
                               for si in range(n_sub)], axis=0)
        c.update(cross=cross, a_mid=jnp.where(upper, qq * jnp.exp2(b2 - mid), 0.0).astype(BF16),
                 b_mid=jnp.where(upper, 0.0, kk * jnp.exp2(mid - b2)).astype(BF16))
        return c

    def stage1(c, st):
        c["o_inter"] = [nt(c["qe"][:, hb[h]], st[h].astype(BF16)) for h in range(heads)]
        st_add = [lax.dot_general(c["vb"][:, hb[h]], c["kdec"][:, hb[h]], TN_DIMS, preferred_element_type=F32)
                  for h in range(heads)]
        c["cross_sc"] = [[nt(a[:, hb[h]], bm[:, hb[h]]) for h in range(heads)] for a, bm in c["cross"]]
        c["mid_sc"] = [nt(c["a_mid"][:, hb[h]], c["b_mid"][:, hb[h]]) for h in range(heads)]
        return [c["e_last"][:, hb[h]] * st[h] + st_add[h] for h in range(heads)]

    def pairwise(c):
        diag = []
        for si in range(n_sub):
            lo_r, hi_r = si * sub, (si + 1) * sub
            qs, ks, vs, bs2 = c["qq"][lo_r:hi_r], c["kk"][lo_r:hi_r], c["v"][lo_r:hi_r], c["b2"][lo_r:hi_r]
            acc = None
            for dlt in range(half):
                k_s, v_s, b_s = (x if dlt == 0 else pltpu.roll(x, dlt, 0) for x in (ks, vs, bs2))
                y = qs * k_s * jnp.exp2(bs2 - b_s)
                term = per_head(lambda h: jnp.where(sub_col % half >= dlt,
                                                    jnp.sum(y[:, hb[h]], axis=-1, keepdims=True), 0.0) * v_s[:, hb[h]])
                acc = term if acc is None else acc + term
            diag.append(acc)
        return jnp.concatenate(diag, axis=0) if n_sub > 1 else diag[0]

    def stage2(c):
        vb = c["vb"]
        o = per_head(lambda h: c["o_inter"][h] + _dot(jnp.where(same_sub, c["mid_sc"][h], 0.0).astype(BF16),
                                                      vb[:, hb[h]]))
        if n_sub > 1:
            zero = jnp.zeros((sub, width), F32)
            o = o + jnp.concatenate([zero] + [per_head(lambda h, si=si: _dot(c["cross_sc"][si - 1][h].astype(BF16),
                                                                            vb[:si * sub, hb[h]]))
                                              for si in range(1, n_sub)], axis=0)
        return o

    def finish(c, o):
        o = per_head(lambda h: o[:, hb[h]] * lax.rsqrt(jnp.mean(o[:, hb[h]] * o[:, hb[h]], axis=-1, keepdims=True)
                                                       + RMS_EPS))
        y_ref[c["ib"], c["rows"], :] = (o * ng * _silu(og_ref[c["ib"], c["rows"], :])).astype(y_ref.dtype)

    def do_chunks(ib, row_slices):
        ctxs = [prep(ib, rows) for rows in row_slices]
        st = [st_scr[ib, h] for h in range(heads)]
        for c in ctxs:
            st = stage1(c, st)
        outs = []
        for c in ctxs:
            d = pairwise(c)
            outs.append(stage2(c) + d)
        for c, o in zip(ctxs, outs):
            finish(c, o)
        for h in range(heads):
            st_scr[ib, h] = st[h]

    group = 4 if n_chunks % 4 == 0 else 1
    for ib in range(tb):
        if n_chunks == group:
            do_chunks(ib, [slice(k * chunk, (k + 1) * chunk) for k in range(group)])
        else:
            def body(c, carry, ib=ib):
                base = c * (group * chunk)
                do_chunks(ib, [pl.ds(pl.multiple_of(base + k * chunk, chunk), chunk) for k in range(group)])
                return carry
            lax.fori_loop(0, n_chunks // group, body, 0)

    @pl.when(t_idx == pl.num_programs(1) - 1)
    def _():
        for ib in range(tb):
            for h in range(heads):
                sout_ref[0, ib, h] = st_scr[ib, h].T


def _hgrn(proj, s0, l_in, lb, norm_g, cols, layer, depth, stacked):
    nb, g, _ = proj.shape
    hg_heads, kd, vd = s0.shape[2:]
    width = hg_heads * kd
    extra = (stacked,)
    if g >= 64:
        tb, tt = 1, _largest_divisor(g, 512, 64)
        chunk, sub = 64, 8
    else:
        tb, tt = _largest_divisor(nb, 8, 1), g
        chunk, sub = g, g
    col = lambda c0: pl.BlockSpec((tb, tt, width), lambda i, t: (i, t, c0))
    return pl.pallas_call(
        functools.partial(_hgrn_kernel, chunk=chunk, sub=sub),
        grid=(nb // tb, g // tt),
        in_specs=[col(cols[0]), col(cols[1]), col(cols[2]), col(cols[3]),
                  pl.BlockSpec((1, width), lambda i, t: (0, 0)),
                  pl.BlockSpec((1, width), lambda i, t: (0, 0)),
                  pl.BlockSpec((1, tb, hg_heads, kd, vd), lambda i, t: (l_in, i, 0, 0, 0))]
                 + [pl.BlockSpec(memory_space=pl.ANY)] * len(extra),
        out_specs=[pl.BlockSpec((tb, tt, width), lambda i, t: (i, t, 0)),
                   pl.BlockSpec((1, tb, hg_heads, kd, vd), lambda i, t: (layer, i, 0, 0, 0))],
        out_shape=[jax.ShapeDtypeStruct((nb, g, width), BF16),
                   jax.ShapeDtypeStruct((depth, nb, hg_heads, kd, vd), F32)],
        input_output_aliases={7 + i: 1 for i in range(len(extra))},
        scratch_shapes=[pltpu.VMEM((tb, hg_heads, vd, kd), F32)],
        compiler_params=_params(("parallel", "arbitrary")),
        name="hgrn",
    )(proj, proj, proj, proj, lb.reshape(1, -1), norm_g.reshape(1, -1), s0, *extra)


def _gelu(x):
    return 0.5 * x * (1.0 + lax.erf(x * (2.0 ** -0.5)))


def _mixffn_kernel(ya_ref, yb_ref, ga_ref, gb_ref, x_ref, gm_ref, shf_ref, scf_ref, gf_ref,
                   wa_ref, wb_ref, wo_ref, l1g_ref, l1b_ref, wu_ref, cw_ref, cb_ref, wd_ref, l2g_ref, l2b_ref,
                   prev_ref, x2_ref, cnew_ref, act_scr, carry_scr, *, alpha, tf):
    tb, tg, d = x_ref.shape
    tm = tb * tg
    dff = wd_ref.shape[0]
    first = pl.program_id(1) == 0

    y_a = _dot(ya_ref[...].reshape(tm, -1), wa_ref[...])
    y_b = _dot(yb_ref[...].reshape(tm, -1), wb_ref[...])
    merged = _sigmoid(ga_ref[...].reshape(tm, d)) * y_a + _sigmoid(gb_ref[...].reshape(tm, d)) * y_b
    z = _dot(merged.astype(BF16), wo_ref[...]).reshape(tb, tg, d)
    x1 = _layer_norm(alpha * x_ref[...] + (1.0 + gm_ref[...]) * z, l1g_ref[...], l1b_ref[...])
    h = (x1 * (1.0 + scf_ref[...]) + shf_ref[...]).reshape(tm, d).astype(BF16)

    @pl.when(first)
    def _():
        carry_scr[...] = jnp.zeros_like(carry_scr)

    tt = lax.broadcasted_iota(jnp.int32, (tb, tg, tf), 1)
    for c in range(dff // tf):
        cols = slice(c * tf, (c + 1) * tf)
        a = _dot(h, wu_ref[:, cols])
        gate = _dot(h, wu_ref[:, dff + c * tf:dff + (c + 1) * tf])
        a3 = a.reshape(tb, tg, tf)
        tail = carry_scr[:, :, cols]
        p0 = jnp.where(first, prev_ref[:, 0:1, cols], tail[:, SUBLANES - 2:SUBLANES - 1, :])
        p1 = jnp.where(first, prev_ref[:, 1:2, cols], tail[:, SUBLANES - 1:SUBLANES, :])
        am1 = jnp.where(tt == 0, p1, pltpu.roll(a, 1, 0).reshape(tb, tg, tf))
        am2 = jnp.where(tt == 0, p0, jnp.where(tt == 1, p1, pltpu.roll(a, 2, 0).reshape(tb, tg, tf)))
        conv = cb_ref[:, cols] + cw_ref[0:1, cols] * am2 + cw_ref[1:2, cols] * am1 + cw_ref[2:3, cols] * a3
        act_scr[:, cols] = (_gelu(conv) * gate.reshape(tb, tg, tf)).astype(BF16).reshape(tm, tf)
        carry_scr[:, :, cols] = a3[:, tg - SUBLANES:, :]
        cnew_ref[:, 0, :, cols] = a3[:, tg - 2:, :]

    y = _dot(act_scr[...], wd_ref[...]).reshape(tb, tg, d)
    x2_ref[...] = _layer_norm(alpha * x1 + (1.0 + gf_ref[...]) * y, l2g_ref[...], l2b_ref[...])


def _mixffn(attn, yb, proj, x3, ada3, prev, w_br_a, w_br_b, w_o, ln1_g, ln1_b, w_up, conv_w, conv_b, w_down,
            ln2_g, ln2_b, alpha, gate_cols, rows=512, tf=256):
    nb, g, d = x3.shape
    dff = w_down.shape[0]
    tb, tg = _row_tiles(nb, g, rows)
    ca, cb = gate_cols
    tok = lambda w, c: pl.BlockSpec((tb, tg, w), lambda i, t: (i, t, c))
    mod = lambda c: pl.BlockSpec((tb, 1, d), lambda i, t: (i, 0, c))
    resident = lambda shape: pl.BlockSpec(shape, lambda i, t: (0,) * len(shape), pipeline_mode=pl.Buffered(1))
    vec = lambda a: a.reshape(1, -1)
    return pl.pallas_call(
        functools.partial(_mixffn_kernel, alpha=alpha, tf=tf),
        grid=(nb // tb, g // tg),
        in_specs=[tok(attn.shape[-1], 0), tok(yb.shape[-1], 0), tok(d, ca), tok(d, cb), tok(d, 0),
                  mod(2), mod(3), mod(4), mod(5),
                  resident(w_br_a.shape), resident(w_br_b.shape), resident(w_o.shape),
                  resident((1, d)), resident((1, d)),
                  resident(w_up.shape), resident(conv_w.shape), resident((1, dff)), resident(w_down.shape),
                  resident((1, d)), resident((1, d)),
                  pl.BlockSpec((tb, 2, dff), lambda i, t: (i, 0, 0))],
        out_specs=[tok(d, 0),
                   pl.BlockSpec((tb, 1, 2, dff), lambda i, t: (i, t, 0, 0))],
        out_shape=[jax.ShapeDtypeStruct((nb, g, d), F32), jax.ShapeDtypeStruct((nb, g // tg, 2, dff), F32)],
        scratch_shapes=[pltpu.VMEM((tb * tg, dff), BF16), pltpu.VMEM((tb, SUBLANES, dff), F32)],
        compiler_params=_params(("parallel", "arbitrary")),
        name="mixffn",
    )(attn, yb, proj, proj, x3, ada3, ada3, ada3, ada3, w_br_a, w_br_b, w_o, vec(ln1_g), vec(ln1_b),
      w_up, conv_w, vec(conv_b), w_down, vec(ln2_g), vec(ln2_b), prev)


def kernel(x_prompt, x_sample, cache_k, cache_v, cache_logf, state_hgrn, state_conv, page_table, c_prompt, c_sample, ln0_g, ln0_b, w_in, fox_f_bias, hg_lb, hg_norm_g, w_br_a, w_br_b, w_o, ln1_g, ln1_b, w_up, conv_w, conv_b, w_down, ln2_g, ln2_b, w_ada, b_ada):
    bp, tp, d = x_prompt.shape
    bs, ts, _ = x_sample.shape
    depth = w_in.shape[0]
    heads, dh = cache_k.shape[3], cache_k.shape[4]
    hw = heads * dh
    hg_heads, kd, vd = state_hgrn.shape[2:]
    hgw = hg_heads * kd
    dff = w_down.shape[1]
    assert conv_w.shape[1] == 3 and state_conv.shape[2] == 2
    assert kd == LANES and vd == LANES and 2 * dh == LANES and hw == hgw == hg_heads * vd and d == 2 * hw
    alpha = (2 * depth) ** 0.25

    lb_soft = jax.nn.softmax(hg_lb.astype(F32), axis=0)
    lb_all = jnp.cumsum(lb_soft, axis=0) - lb_soft[0]

    o_f = 3 * hw
    o_b = o_f + heads
    o_g = o_b + 4 * hgw
    w_main = jnp.concatenate([w_in[:, :, o_g:], w_in[:, :, o_b:o_g], w_in[:, :, :o_f]], axis=2).astype(BF16)
    w_f = jnp.pad(w_in[:, :, o_f:o_b], ((0, 0), (0, 0), (0, LANES - heads))).astype(BF16)
    f_bias = jnp.pad(fox_f_bias, ((0, 0), (0, LANES - heads))).reshape(depth, 1, LANES)
    w_rest = 2 * d + 4 * hgw
    gate_cols = (0, 1)
    hg_cols = tuple(2 * d // hgw + i for i in range(4))
    qkv_cols = tuple(w_rest // hw + i for i in range(3))
    k_off, v_off = w_rest + hw, w_rest + 2 * hw
    w_br_a16, w_br_b16, w_o16 = w_br_a.astype(BF16), w_br_b.astype(BF16), w_o.astype(BF16)
    w_up16, w_down16 = w_up.astype(BF16), w_down.astype(BF16)
    consts = _aug_constants(heads, dh)

    ada = _ada(jnp.concatenate([c_prompt, c_sample], axis=0), w_ada, b_ada)
    xp = _ln0(x_prompt.reshape(bp * tp, d), ln0_g, ln0_b).reshape(bp, tp, d)
    xs = _ln0(x_sample.reshape(bs * ts, d), ln0_g, ln0_b).reshape(bs, ts, d)
    s0_p = jnp.zeros((1, bp) + state_hgrn.shape[2:], F32)
    conv0_p = jnp.zeros((bp, 2, dff), F32)

    kv_t = [jnp.zeros((depth, bp, hw, tp), F32), jnp.zeros((depth, bp, hw, tp), F32)]
    hg_p = jnp.zeros((depth, bp) + state_hgrn.shape[2:], F32)
    hg_s = jnp.zeros((depth, bs) + state_hgrn.shape[2:], F32)
    outs_p, outs_s = [], []
    for l in range(depth):
        ada_p = ada[l, :bp].reshape(bp, 1, 6 * d)
        ada_s = ada[l, bp:].reshape(bs, 1, 6 * d)

        proj, logf = _inproj(xp, ada_p, w_main[l], w_f[l], f_bias[l], width=w_rest)
        qa, ka, va_t, *kv_t = _attn_prep(xp, ada_p, w_main[l], logf, consts, qkv_cols, heads, dh, l, depth, kv_t)
        attn = _attn_prompt(qa, ka, va_t, heads, dh)
        yb, hg_p = _hgrn(proj, s0_p, 0, lb_all[l], hg_norm_g[l], hg_cols, l, depth, hg_p)
        xp, conv_new = _mixffn(attn, yb, proj, xp, ada_p, conv0_p, w_br_a16[l], w_br_b16[l], w_o16[l],
                               ln1_g[l], ln1_b[l], w_up16[l], conv_w[l], conv_b[l], w_down16[l],
                               ln2_g[l], ln2_b[l], alpha, gate_cols)
        outs_p.append((logf[:, :, :heads], conv_new[:, -1]))

        proj, logf = _inproj(xs, ada_s, w_main[l], w_f[l], f_bias[l], width=w_main.shape[2])
        attn = _attn_sample(proj, logf, cache_k, cache_v, cache_logf, page_table, l, qkv_cols, heads, dh)
        yb, hg_s = _hgrn(proj, state_hgrn, l, lb_all[l], hg_norm_g[l], hg_cols, l, depth, hg_s)
        xs, conv_new = _mixffn(attn, yb, proj, xs, ada_s, state_conv[l], w_br_a16[l], w_br_b16[l], w_o16[l],
                               ln1_g[l], ln1_b[l], w_up16[l], conv_w[l], conv_b[l], w_down16[l],
                               ln2_g[l], ln2_b[l], alpha, gate_cols)
        outs_s.append((proj[:, :, k_off:k_off + hw].reshape(bs, ts, heads, dh),
                       proj[:, :, v_off:v_off + hw].reshape(bs, ts, heads, dh),
                       logf[:, :, :heads], conv_new[:, -1]))

    stack = lambda outs, i: jnp.stack([o[i] for o in outs])
    k_p, v_p = (jnp.transpose(a.reshape(depth, bp, heads, dh, tp), (0, 1, 4, 2, 3)) for a in kv_t)
    return (xp, xs, k_p, v_p, stack(outs_p, 0), hg_p, stack(outs_p, 1),
            stack(outs_s, 0), stack(outs_s, 1), stack(outs_s, 2), hg_s, stack(outs_s, 3))
```

```python
import functools

import numpy as np
import jax
import jax.numpy as jnp
from jax import lax
from jax.experimental import pallas as pl
from jax.experimental.pallas import tpu as pltpu

F32 = jnp.float32
BF16 = jnp.bfloat16

LN_EPS = 1e-5
RMS_EPS = 1e-6
MASK_VALUE = -1e30
LOG2_E = 1.4426950408889634
LANES = 128
SUBLANES = 8
VMEM_LIMIT = 56 * 1024 * 1024

NT_DIMS = (((1,), (1,)), ((), ()))
TN_DIMS = (((0,), (0,)), ((), ()))


def _params(sem):
    return pltpu.CompilerParams(dimension_semantics=sem, vmem_limit_bytes=VMEM_LIMIT)


def _largest_divisor(n, target, mult):
    best = None
    for d in range(1, n + 1):
        if n % d == 0 and d <= target and d % mult == 0:
            best = d
    return n if best is None else best


def _row_tiles(nb, g, target):
    if g >= target:
        return 1, _largest_divisor(g, target, SUBLANES)
    return _largest_divisor(nb, max(1, target // g), 1), g


def _split3(x):
    hi = x.astype(BF16)
    r1 = x - hi.astype(F32)
    mid = r1.astype(BF16)
    lo = (r1 - mid.astype(F32)).astype(BF16)
    return hi, mid, lo


def _dot(a, b):
    return jnp.dot(a, b, preferred_element_type=F32)


def _dot3(a_bf16, x_f32):
    hi, mid, lo = _split3(x_f32)
    return _dot(a_bf16, hi) + _dot(a_bf16, mid) + _dot(a_bf16, lo)


def _layer_norm(x, g, b):
    mu = jnp.mean(x, axis=-1, keepdims=True)
    xc = x - mu
    var = jnp.mean(xc * xc, axis=-1, keepdims=True)
    return xc * lax.rsqrt(var + LN_EPS) * g + b


def _sigmoid(x):
    return 1.0 / (1.0 + jnp.exp(-x))


def _silu(x):
    return x * _sigmoid(x)


def _log_sigmoid(x):
    return jnp.minimum(x, 0.0) - jnp.log1p(jnp.exp(-jnp.abs(x)))


def _tri_lower(n):
    r = lax.broadcasted_iota(jnp.int32, (n, n), 0)
    c = lax.broadcasted_iota(jnp.int32, (n, n), 1)
    return jnp.where(r >= c, 1.0, 0.0).astype(BF16)


def _ln0_kernel(x_ref, g_ref, b_ref, o_ref):
    o_ref[...] = _layer_norm(x_ref[...], g_ref[...], b_ref[...])


def _ln0(x2, g, b):
    n, d = x2.shape
    tm = _largest_divisor(n, 1024, SUBLANES)
    return pl.pallas_call(
        _ln0_kernel,
        grid=(n // tm,),
        in_specs=[pl.BlockSpec((tm, d), lambda i: (i, 0)),
                  pl.BlockSpec((1, d), lambda i: (0, 0)),
                  pl.BlockSpec((1, d), lambda i: (0, 0))],
        out_specs=pl.BlockSpec((tm, d), lambda i: (i, 0)),
        out_shape=jax.ShapeDtypeStruct((n, d), F32),
        compiler_params=_params(("parallel",)),
        name="ln0",
    )(x2, g.reshape(1, d), b.reshape(1, d))


def _ada_kernel(c_ref, w_ref, b_ref, o_ref):
    s = _silu(c_ref[...]).astype(BF16)
    o_ref[0] = _dot(s, w_ref[0].astype(BF16)) + b_ref[0]


def _ada(c_all, w_ada, b_ada):
    depth, d, n6 = w_ada.shape
    nc = c_all.shape[0]
    tn = _largest_divisor(n6, 1536, LANES)
    return pl.pallas_call(
        _ada_kernel,
        grid=(depth, n6 // tn),
        in_specs=[pl.BlockSpec((nc, d), lambda l, j: (0, 0)),
                  pl.BlockSpec((1, d, tn), lambda l, j: (l, 0, j)),
                  pl.BlockSpec((1, 1, tn), lambda l, j: (l, 0, j))],
        out_specs=pl.BlockSpec((1, nc, tn), lambda l, j: (l, 0, j)),
        out_shape=jax.ShapeDtypeStruct((depth, nc, n6), F32),
        compiler_params=_params(("parallel", "parallel")),
        name="ada",
    )(c_all, w_ada, b_ada.reshape(depth, 1, n6))


def _inproj_kernel(x_ref, sh_ref, sc_ref, w_ref, wf_ref, fb_ref, proj_ref, logf_ref, *, tn):
    tb, tg, d = x_ref.shape
    h = (x_ref[...] * (1.0 + sc_ref[...]) + sh_ref[...]).reshape(tb * tg, d).astype(BF16)
    f = _dot(h, wf_ref[...]) + fb_ref[...]
    logf_ref[...] = _log_sigmoid(f).reshape(tb, tg, LANES)
    for c in range(w_ref.shape[1] // tn):
        cols = slice(c * tn, (c + 1) * tn)
        proj_ref[:, :, cols] = _dot(h, w_ref[:, cols]).reshape(tb, tg, tn)


def _inproj(x3, ada3, w_main, w_f, f_bias, width, tn=512, rows=512):
    nb, g, d = x3.shape
    tb, tg = _row_tiles(nb, g, rows)
    resident = lambda shape: pl.BlockSpec(shape, lambda i, t: (0,) * len(shape), pipeline_mode=pl.Buffered(1))
    return pl.pallas_call(
        functools.partial(_inproj_kernel, tn=tn),
        grid=(nb // tb, g // tg),
        in_specs=[pl.BlockSpec((tb, tg, d), lambda i, t: (i, t, 0)),
                  pl.BlockSpec((tb, 1, d), lambda i, t: (i, 0, 0)),
                  pl.BlockSpec((tb, 1, d), lambda i, t: (i, 0, 1)),
                  resident((d, width)),
                  resident((d, LANES)),
                  resident((1, LANES))],
        out_specs=[pl.BlockSpec((tb, tg, width), lambda i, t: (i, t, 0)),
                   pl.BlockSpec((tb, tg, LANES), lambda i, t: (i, t, 0))],
        out_shape=[jax.ShapeDtypeStruct((nb, g, width), F32),
                   jax.ShapeDtypeStruct((nb, g, LANES), F32)],
        compiler_params=_params(("parallel", "parallel")),
        name="inproj",
    )(x3, ada3, ada3, w_main, w_f, f_bias)


def _aug_constants(heads, dh):
    width = heads * LANES
    pq = np.zeros((LANES, width), np.float32)
    pk = np.zeros((LANES, width), np.float32)
    ones_q = np.zeros((1, width), np.float32)
    ones_k = np.zeros((1, width), np.float32)
    own = np.zeros((1, width), np.float32)
    for h in range(heads):
        o_h = (h % 2) * dh
        a_h = h * LANES + (dh - o_h)
        own[0, h * LANES + o_h:h * LANES + o_h + dh] = 1.0
        for i in range(3):
            pq[i * heads + h, a_h + i] = 1.0
            pk[i * heads + h, a_h + 3 + i] = 1.0
            ones_q[0, a_h + 3 + i] = 1.0
            ones_k[0, a_h + i] = 1.0
    return (jnp.asarray(pq, BF16), jnp.asarray(pk, BF16), jnp.asarray(ones_q), jnp.asarray(ones_k),
            jnp.asarray(own))


def _prep_kernel(x_ref, sh_ref, sc_ref, wq_ref, wk_ref, wv_ref, lf_ref, pq_ref, pk_ref, oq_ref, ok_ref, own_ref,
                 *rest, scale):
    qa_ref, ka_ref, va_ref, kt_ref, vt_ref, carry = rest[-6:]
    tc = x_ref.shape[1]
    heads = qa_ref.shape[2] // LANES

    @pl.when(pl.program_id(1) == 0)
    def _():
        carry[...] = jnp.zeros_like(carry)

    h = (x_ref[0] * (1.0 + sc_ref[0]) + sh_ref[0]).astype(BF16)
    q = _dot(h, wq_ref[...])
    k = _dot(h, wk_ref[...])
    v = _dot(h, wv_ref[...])

    lane = lax.broadcasted_iota(jnp.int32, (tc, LANES), 1)

    def pack3(x):
        pieces = [jnp.where(lane < heads, p.astype(F32), 0.0) for p in _split3(x)]
        return (pieces[0] + pltpu.roll(pieces[1], heads, 1) + pltpu.roll(pieces[2], 2 * heads, 1)).astype(BF16)

    parts = _dot(_tri_lower(tc), pack3(lf_ref[0]))
    d = parts + pltpu.roll(parts, LANES - heads, 1) + pltpu.roll(parts, LANES - 2 * heads, 1) + carry[...]
    carry[...] = d[tc - 1:tc, :]
    packed = pack3(d * LOG2_E)
    aux_q = _dot(packed, pq_ref[...]) + oq_ref[...]
    aux_k = ok_ref[...] - _dot(packed, pk_ref[...])
    own = own_ref[...] > 0.5

    def widen(x):
        return jnp.concatenate([x[:, (h // 2) * LANES:(h // 2 + 1) * LANES] for h in range(heads)], axis=1)

    qa_ref[0] = jnp.where(own, widen(q) * (scale * LOG2_E), aux_q).astype(BF16)
    ka_ref[0] = jnp.where(own, widen(k), aux_k).astype(BF16)
    dh = LANES // 2
    row = lax.broadcasted_iota(jnp.int32, (LANES, tc), 0)
    for hp in range(heads // 2):
        pair = slice(hp * LANES, (hp + 1) * LANES)
        kt_ref[0, 0, pair, :] = k[:, pair].T
        vt = v[:, pair].T
        vt_ref[0, 0, pair, :] = vt
        va_ref[0, (2 * hp) * LANES:(2 * hp + 1) * LANES, :] = jnp.where(
            row < dh, vt, jnp.where(row == dh, 1.0, 0.0)).astype(BF16)
        va_ref[0, (2 * hp + 1) * LANES:(2 * hp + 2) * LANES, :] = jnp.where(
            row >= dh, vt, jnp.where(row == 0, 1.0, 0.0)).astype(BF16)


def _attn_prep(x3, ada3, w_main, logf, consts, cols, heads, dh, layer, depth, stacked):
    b, t, d = x3.shape
    hw = heads * dh
    width = heads * LANES
    tc = _largest_divisor(t, 512, SUBLANES)
    pq, pk, oq, ok, own = consts
    cq, ck, cv = cols
    full = lambda shape: pl.BlockSpec(shape, lambda i, j: (0,) * len(shape))
    out = jax.ShapeDtypeStruct((b, t, width), BF16)
    cache = jax.ShapeDtypeStruct((depth, b, hw, t), F32)
    cache_spec = pl.BlockSpec((1, 1, hw, tc), lambda i, j: (layer, i, 0, j))
    extra = tuple(stacked)
    n_in = 12
    return pl.pallas_call(
        functools.partial(_prep_kernel, scale=dh ** -0.5),
        grid=(b, t // tc),
        in_specs=[pl.BlockSpec((1, tc, d), lambda i, j: (i, j, 0)),
                  pl.BlockSpec((1, 1, d), lambda i, j: (i, 0, 0)),
                  pl.BlockSpec((1, 1, d), lambda i, j: (i, 0, 1)),
                  pl.BlockSpec((d, hw), lambda i, j: (0, cq)),
                  pl.BlockSpec((d, hw), lambda i, j: (0, ck)),
                  pl.BlockSpec((d, hw), lambda i, j: (0, cv)),
                  pl.BlockSpec((1, tc, LANES), lambda i, j: (i, j, 0)),
                  full(pq.shape), full(pk.shape), full(oq.shape), full(ok.shape), full(own.shape)]
                 + [pl.BlockSpec(memory_space=pl.ANY)] * len(extra),
        out_specs=[pl.BlockSpec((1, tc, width), lambda i, j: (i, j, 0)),
                   pl.BlockSpec((1, tc, width), lambda i, j: (i, j, 0)),
                   pl.BlockSpec((1, width, tc), lambda i, j: (i, 0, j)),
                   cache_spec, cache_spec],
        out_shape=[out, out, jax.ShapeDtypeStruct((b, width, t), BF16), cache, cache],
        input_output_aliases={n_in + i: 3 + i for i in range(len(extra))},
        scratch_shapes=[pltpu.VMEM((1, LANES), F32)],
        compiler_params=_params(("parallel", "arbitrary")),
        name="attn_prep",
    )(x3, ada3, ada3, w_main, w_main, w_main, logf, pq, pk, oq, ok, own, *extra)


def _attn_kernel(qi_ref, kj_ref, q_ref, k_ref, v_ref, o_ref, m_scr, acc_scr, *, dh):
    tq = q_ref.shape[1]
    tk = k_ref.shape[1]
    heads = q_ref.shape[2] // LANES
    step = pl.program_id(1)
    i = qi_ref[step]
    j = kj_ref[step]

    @pl.when(j == 0)
    def _():
        m_scr[...] = jnp.full_like(m_scr, MASK_VALUE)
        acc_scr[...] = jnp.zeros_like(acc_scr)

    def update(masked):
        if masked:
            key = lax.broadcasted_iota(jnp.int32, (tk, tq), 0)
            qry = lax.broadcasted_iota(jnp.int32, (tk, tq), 1)
            keep = key <= qry

        def scores(h):
            blk = slice(h * LANES, (h + 1) * LANES)
            s = lax.dot_general(k_ref[0, :, blk], q_ref[0, :, blk], NT_DIMS, preferred_element_type=F32)
            if masked:
                s = jnp.where(keep, s, MASK_VALUE)
            m_prev = m_scr[h]
            return s, m_prev, jnp.maximum(m_prev, jnp.max(s, axis=0, keepdims=True))

        def accumulate(h, s, m_prev, m_new):
            p = jnp.exp2(s - m_new).astype(BF16)
            acc_scr[h] = jnp.exp2(m_prev - m_new) * acc_scr[h] + _dot(v_ref[0, h * LANES:(h + 1) * LANES, :], p)
            m_scr[h] = m_new

        ahead = 2
        pending = [scores(h) for h in range(ahead)]
        for h in range(heads):
            if h + ahead < heads:
                pending.append(scores(h + ahead))
            accumulate(h, *pending.pop(0))

    @pl.when(j < i)
    def _():
        update(False)

    @pl.when(j == i)
    def _():
        update(True)
        row = lax.broadcasted_iota(jnp.int32, (LANES, tq), 0)
        for hp in range(heads // 2):
            a0 = acc_scr[2 * hp]
            a1 = acc_scr[2 * hp + 1]
            o_t = jnp.where(row < dh, a0 / a0[dh:dh + 1, :], a1 / a1[0:1, :])
            o_ref[0, :, hp * LANES:(hp + 1) * LANES] = o_t.T.astype(o_ref.dtype)


def _attn_prompt(qa, ka, va, heads, dh):
    b, t, width = qa.shape
    tq = _largest_divisor(t, 512, SUBLANES)
    nq = t // tq
    qi = np.array([i for i in range(nq) for j in range(i + 1)], np.int32)
    kj = np.array([j for i in range(nq) for j in range(i + 1)], np.int32)
    grid_spec = pltpu.PrefetchScalarGridSpec(
        num_scalar_prefetch=2,
        grid=(b, len(qi)),
        in_specs=[pl.BlockSpec((1, tq, width), lambda bb, s, qi_r, kj_r: (bb, qi_r[s], 0)),
                  pl.BlockSpec((1, tq, width), lambda bb, s, qi_r, kj_r: (bb, kj_r[s], 0)),
                  pl.BlockSpec((1, width, tq), lambda bb, s, qi_r, kj_r: (bb, 0, kj_r[s]))],
        out_specs=pl.BlockSpec((1, tq, heads * dh), lambda bb, s, qi_r, kj_r: (bb, qi_r[s], 0)),
        scratch_shapes=[pltpu.VMEM((heads, 1, tq), F32), pltpu.VMEM((heads, LANES, tq), F32)],
    )
    return pl.pallas_call(
        functools.partial(_attn_kernel, dh=dh),
        grid_spec=grid_spec,
        out_shape=jax.ShapeDtypeStruct((b, t, heads * dh), BF16),
        compiler_params=_params(("parallel", "arbitrary")),
        name="attn_prompt",
    )(jnp.asarray(qi), jnp.asarray(kj), qa, ka, va)


def _attn_sample_kernel(pt_ref, q_ref, kc_ref, vc_ref, lfc_ref, ck_hbm, cv_hbm, clf_hbm, o_ref,
                        kbuf, vbuf, lbuf, sems, *, layer, n_pages, heads, dh):
    b = pl.program_id(0)
    nb = pl.num_programs(0)
    page = lbuf.shape[2]
    hw = heads * dh
    t_cur = q_ref.shape[1]
    rows = heads * t_cur
    npr = n_pages * heads

    def copies(seq, slot):
        out = []
        for p in range(n_pages):
            pid = pt_ref[seq, p]
            lanes = pl.ds(p * page, page)
            out.append(pltpu.make_async_copy(ck_hbm.at[layer, pid], kbuf.at[slot, :, lanes], sems.at[0, slot, p]))
            out.append(pltpu.make_async_copy(cv_hbm.at[layer, pid], vbuf.at[slot, :, lanes], sems.at[1, slot, p]))
            out.append(pltpu.make_async_copy(clf_hbm.at[layer, pid], lbuf.at[slot, pl.ds(p * heads, heads), :],
                                             sems.at[2, slot, p]))
        return out

    @pl.when(b == 0)
    def _():
        for c in copies(0, 0):
            c.start()

    slot = b % 2

    @pl.when(b + 1 < nb)
    def _():
        for c in copies(b + 1, 1 - slot):
            c.start()

    for c in copies(b, slot):
        c.wait()

    ur = lax.broadcasted_iota(jnp.int32, (page, page), 0)
    uc = lax.broadcasted_iota(jnp.int32, (page, page), 1)
    triu = jnp.where(ur <= uc, 1.0, 0.0).astype(BF16)
    l_hi, l_mid, l_lo = _split3(lbuf[slot])
    cum = _dot(l_hi, triu) + _dot(l_mid, triu) + _dot(l_lo, triu)
    mr = lax.broadcasted_iota(jnp.int32, (npr, npr), 0)
    mc = lax.broadcasted_iota(jnp.int32, (npr, npr), 1)
    later = jnp.where((mr % heads == mc % heads) & (mc // heads >= mr // heads), 1.0, 0.0).astype(BF16)
    e = cum - _dot3(later, jnp.broadcast_to(cum[:, page - 1:page], (npr, page)))
    rr = lax.broadcasted_iota(jnp.int32, (n_pages * rows, npr), 0)
    rc = lax.broadcasted_iota(jnp.int32, (n_pages * rows, npr), 1)
    spread = jnp.where((rr // rows == rc // heads) & ((rr % rows) // t_cur == rc % heads), -1.0, 0.0).astype(BF16)
    bias_rows = _dot3(spread, e)
    bias = jnp.concatenate([bias_rows[p * rows:(p + 1) * rows] for p in range(n_pages)], axis=1)

    cc = _dot3(_tri_lower(t_cur), lfc_ref[0])
    rrow = lax.broadcasted_iota(jnp.int32, (rows, hw), 0) // t_cur
    rcol = lax.broadcasted_iota(jnp.int32, (rows, hw), 1) // dh
    q_rep = jnp.concatenate([q_ref[0]] * heads, axis=0)
    qbd = jnp.where(rrow == rcol, q_rep * (dh ** -0.5), 0.0).astype(BF16)
    arow = lax.broadcasted_iota(jnp.int32, (rows, LANES), 0) // t_cur
    acol = lax.broadcasted_iota(jnp.int32, (rows, LANES), 1)
    sel = arow == acol
    neg_sel = jnp.where(sel, -1.0, 0.0).astype(BF16)
    cc_rep = jnp.concatenate([cc] * heads, axis=0)
    dq = jnp.sum(jnp.where(sel, cc_rep, 0.0), axis=-1, keepdims=True)

    nt = lambda a, bm: lax.dot_general(a, bm, NT_DIMS, preferred_element_type=F32)
    s_past = _dot(qbd, kbuf[slot].astype(BF16)) + bias + dq
    c_hi, c_mid, c_lo = _split3(cc)
    s_cur = nt(qbd, kc_ref[0].astype(BF16)) + (nt(neg_sel, c_hi) + nt(neg_sel, c_mid) + nt(neg_sel, c_lo)) + dq
    tq_idx = lax.broadcasted_iota(jnp.int32, (rows, t_cur), 0) % t_cur
    ts_idx = lax.broadcasted_iota(jnp.int32, (rows, t_cur), 1)
    s_cur = jnp.where(ts_idx <= tq_idx, s_cur, MASK_VALUE)

    m = jnp.maximum(jnp.max(s_past, axis=-1, keepdims=True), jnp.max(s_cur, axis=-1, keepdims=True))
    p_past = jnp.exp(s_past - m)
    p_cur = jnp.exp(s_cur - m)
    denom = jnp.sum(p_past, axis=-1, keepdims=True) + jnp.sum(p_cur, axis=-1, keepdims=True)
    o_all = (nt(p_past.astype(BF16), vbuf[slot].astype(BF16))
             + _dot(p_cur.astype(BF16), vc_ref[0].astype(BF16))) / denom
    o_all = jnp.where(rrow == rcol, o_all, 0.0)
    o = o_all[0:t_cur]
    for h in range(1, heads):
        o = o + o_all[h * t_cur:(h + 1) * t_cur]
    o_ref[0] = o.astype(o_ref.dtype)


def _attn_sample(proj, logf, cache_k, cache_v, cache_logf, page_table, layer, cols, heads, dh):
    nb, t_cur, _ = proj.shape
    hw = heads * dh
    depth, n_pool, page = cache_k.shape[:3]
    n_pages = page_table.shape[1]
    ck = jnp.transpose(cache_k, (0, 1, 3, 4, 2)).reshape(depth, n_pool, hw, page)
    cv = jnp.transpose(cache_v, (0, 1, 3, 4, 2)).reshape(depth, n_pool, hw, page)
    clf = jnp.transpose(cache_logf, (0, 1, 3, 2))
    cq, ckc, cvc = cols
    grid_spec = pltpu.PrefetchScalarGridSpec(
        num_scalar_prefetch=1,
        grid=(nb,),
        in_specs=[pl.BlockSpec((1, t_cur, hw), lambda b, pt: (b, 0, cq)),
                  pl.BlockSpec((1, t_cur, hw), lambda b, pt: (b, 0, ckc)),
                  pl.BlockSpec((1, t_cur, hw), lambda b, pt: (b, 0, cvc)),
                  pl.BlockSpec((1, t_cur, LANES), lambda b, pt: (b, 0, 0)),
                  pl.BlockSpec(memory_space=pl.ANY),
                  pl.BlockSpec(memory_space=pl.ANY),
                  pl.BlockSpec(memory_space=pl.ANY)],
        out_specs=pl.BlockSpec((1, t_cur, hw), lambda b, pt: (b, 0, 0)),
        scratch_shapes=[pltpu.VMEM((2, hw, n_pages * page), F32),
                        pltpu.VMEM((2, hw, n_pages * page), F32),
                        pltpu.VMEM((2, n_pages * heads, page), F32),
                        pltpu.SemaphoreType.DMA((3, 2, n_pages))],
    )
    return pl.pallas_call(
        functools.partial(_attn_sample_kernel, layer=layer, n_pages=n_pages, heads=heads, dh=dh),
        grid_spec=grid_spec,
        out_shape=jax.ShapeDtypeStruct((nb, t_cur, hw), BF16),
        compiler_params=_params(("arbitrary",)),
        name="attn_sample",
    )(page_table, proj, proj, proj, logf, ck, cv, clf)


def _hgrn_kernel(q_ref, f_ref, v_ref, og_ref, lb_ref, ng_ref, s0_ref, *rest, chunk, sub):
    y_ref, sout_ref, st_scr = rest[-3:]
    tb, tt, width = q_ref.shape
    heads = s0_ref.shape[2]
    kd = width // heads
    n_chunks = tt // chunk
    n_sub = chunk // sub
    t_idx = pl.program_id(1)
    hb = [slice(h * kd, (h + 1) * kd) for h in range(heads)]
    per_head = lambda fn: jnp.concatenate([fn(h) for h in range(heads)], axis=1)

    @pl.when(t_idx == 0)
    def _():
        for ib in range(tb):
            for h in range(heads):
                st_scr[ib, h] = s0_ref[0, ib, h].T

    lb = lb_ref[...]
    ng = ng_ref[...]
    tri = _tri_lower(chunk)
    sub_col = lax.broadcasted_iota(jnp.int32, (sub, 1), 0)
    half = sub // 2
    upper = lax.broadcasted_iota(jnp.int32, (chunk, 1), 0) % sub >= half
    same_sub = (lax.broadcasted_iota(jnp.int32, (chunk, chunk), 0) // sub
                == lax.broadcasted_iota(jnp.int32, (chunk, chunk), 1) // sub)

    nt = lambda x, y: lax.dot_general(x, y, NT_DIMS, preferred_element_type=F32)

    def prep(ib, rows):
        c = {"ib": ib, "rows": rows}
        g = lb + (1.0 - lb) * _sigmoid(f_ref[ib, rows, :])
        kk = 1.0 - g
        qq = _silu(q_ref[ib, rows, :])
        v = v_ref[ib, rows, :]
        bcum = _dot3(tri, jnp.log(g))
        b_last = bcum[chunk - 1:chunk]
        b2 = bcum * LOG2_E
        c.update(kk=kk, qq=qq, v=v, vb=v.astype(BF16), b2=b2, e_last=jnp.exp(b_last),
                 qe=(qq * jnp.exp(bcum)).astype(BF16), kdec=(kk * jnp.exp(b_last - bcum)).astype(BF16))
        cross = []
        for si in range(1, n_sub):
            lo_r, hi_r = si * sub, (si + 1) * sub
            ref = bcum[lo_r - 1:lo_r]
            cross.append(((qq[lo_r:hi_r] * jnp.exp(bcum[lo_r:hi_r] - ref)).astype(BF16),
                          (kk[:lo_r] * jnp.exp(ref - bcum[:lo_r])).astype(BF16)))
        mid = jnp.concatenate([jnp.broadcast_to(b2[si * sub + half - 1:si * sub + half], (sub, width))
                               for si in range(n_sub)], axis=0)
        c.update(cross=cross, a_mid=jnp.where(upper, qq * jnp.exp2(b2 - mid), 0.0).astype(BF16),
                 b_mid=jnp.where(upper, 0.0, kk * jnp.exp2(mid - b2)).astype(BF16))
        return c

    def stage1(c, st):
        c["o_inter"] = [nt(c["qe"][:, hb[h]], st[h].astype(BF16)) for h in range(heads)]
        st_add = [lax.dot_general(c["vb"][:, hb[h]], c["kdec"][:, hb[h]], TN_DIMS, preferred_element_type=F32)
                  for h in range(heads)]
        c["cross_sc"] = [[nt(a[:, hb[h]], bm[:, hb[h]]) for h in range(heads)] for a, bm in c["cross"]]
        c["mid_sc"] = [nt(c["a_mid"][:, hb[h]], c["b_mid"][:, hb[h]]) for h in range(heads)]
        return [c["e_last"][:, hb[h]] * st[h] + st_add[h] for h in range(heads)]

    def pairwise(c):
        diag = []
        for si in range(n_sub):
            lo_r, hi_r = si * sub, (si + 1) * sub
            qs, ks, vs, bs2 = c["qq"][lo_r:hi_r], c["kk"][lo_r:hi_r], c["v"][lo_r:hi_r], c["b2"][lo_r:hi_r]
            acc = None
            for dlt in range(half):
                k_s, v_s, b_s = (x if dlt == 0 else pltpu.roll(x, dlt, 0) for x in (ks, vs, bs2))
                y = qs * k_s * jnp.exp2(bs2 - b_s)
                term = per_head(lambda h: jnp.where(sub_col % half >= dlt,
                                                    jnp.sum(y[:, hb[h]], axis=-1, keepdims=True), 0.0) * v_s[:, hb[h]])
                acc = term if acc is None else acc + term
            diag.append(acc)
        return jnp.concatenate(diag, axis=0) if n_sub > 1 else diag[0]

    def stage2(c):
        vb = c["vb"]
        o = per_head(lambda h: c["o_inter"][h] + _dot(jnp.where(same_sub, c["mid_sc"][h], 0.0).astype(BF16),
                                                      vb[:, hb[h]]))
        if n_sub > 1:
            zero = jnp.zeros((sub, width), F32)
            o = o + jnp.concatenate([zero] + [per_head(lambda h, si=si: _dot(c["cross_sc"][si - 1][h].astype(BF16),
                                                                            vb[:si * sub, hb[h]]))
                                              for si in range(1, n_sub)], axis=0)
        return o

    def finish(c, o):
        o = per_head(lambda h: o[:, hb[h]] * lax.rsqrt(jnp.mean(o[:, hb[h]] * o[:, hb[h]], axis=-1, keepdims=True)
                                                       + RMS_EPS))
        y_ref[c["ib"], c["rows"], :] = (o * ng * _silu(og_ref[c["ib"], c["rows"], :])).astype(y_ref.dtype)

    def do_chunks(ib, row_slices):
        ctxs = [prep(ib, rows) for rows in row_slices]
        st = [st_scr[ib, h] for h in range(heads)]
        for c in ctxs:
            st = stage1(c, st)
        outs = []
        for c in ctxs:
            d = pairwise(c)
            outs.append(stage2(c) + d)
        for c, o in zip(ctxs, outs):
            finish(c, o)
        for h in range(heads):
            st_scr[ib, h] = st[h]

    group = 4 if n_chunks % 4 == 0 else 1
    for ib in range(tb):
        if n_chunks == group:
            do_chunks(ib, [slice(k * chunk, (k + 1) * chunk) for k in range(group)])
        else:
            def body(c, carry, ib=ib):
                base = c * (group * chunk)
                do_chunks(ib, [pl.ds(pl.multiple_of(base + k * chunk, chunk), chunk) for k in range(group)])
                return carry
            lax.fori_loop(0, n_chunks // group, body, 0)

    @pl.when(t_idx == pl.num_programs(1) - 1)
    def _():
        for ib in range(tb):
            for h in range(heads):
                sout_ref[0, ib, h] = st_scr[ib, h].T


def _hgrn(proj, s0, l_in, lb, norm_g, cols, layer, depth, stacked):
    nb, g, _ = proj.shape
    hg_heads, kd, vd = s0.shape[2:]
    width = hg_heads * kd
    extra = (stacked,)
    if g >= 64:
        tb, tt = 1, _largest_divisor(g, 512, 64)
        chunk, sub = 64, 8
    else:
        tb, tt = _largest_divisor(nb, 8, 1), g
        chunk, sub = g, g
    col = lambda c0: pl.BlockSpec((tb, tt, width), lambda i, t: (i, t, c0))
    return pl.pallas_call(
        functools.partial(_hgrn_kernel, chunk=chunk, sub=sub),
        grid=(nb // tb, g // tt),
        in_specs=[col(cols[0]), col(cols[1]), col(cols[2]), col(cols[3]),
                  pl.BlockSpec((1, width), lambda i, t: (0, 0)),
                  pl.BlockSpec((1, width), lambda i, t: (0, 0)),
                  pl.BlockSpec((1, tb, hg_heads, kd, vd), lambda i, t: (l_in, i, 0, 0, 0))]
                 + [pl.BlockSpec(memory_space=pl.ANY)] * len(extra),
        out_specs=[pl.BlockSpec((tb, tt, width), lambda i, t: (i, t, 0)),
                   pl.BlockSpec((1, tb, hg_heads, kd, vd), lambda i, t: (layer, i, 0, 0, 0))],
        out_shape=[jax.ShapeDtypeStruct((nb, g, width), BF16),
                   jax.ShapeDtypeStruct((depth, nb, hg_heads, kd, vd), F32)],
        input_output_aliases={7 + i: 1 for i in range(len(extra))},
        scratch_shapes=[pltpu.VMEM((tb, hg_heads, vd, kd), F32)],
        compiler_params=_params(("parallel", "arbitrary")),
        name="hgrn",
    )(proj, proj, proj, proj, lb.reshape(1, -1), norm_g.reshape(1, -1), s0, *extra)


def _gelu(x):
    return 0.5 * x * (1.0 + lax.erf(x * (2.0 ** -0.5)))


def _mixffn_kernel(ya_ref, yb_ref, ga_ref, gb_ref, x_ref, gm_ref, shf_ref, scf_ref, gf_ref,
                   wa_ref, wb_ref, wo_ref, l1g_ref, l1b_ref, wu_ref, cw_ref, cb_ref, wd_ref, l2g_ref, l2b_ref,
                   prev_ref, x2_ref, cnew_ref, act_scr, carry_scr, *, alpha, tf):
    tb, tg, d = x_ref.shape
    tm = tb * tg
    dff = wd_ref.shape[0]
    first = pl.program_id(1) == 0

    y_a = _dot(ya_ref[...].reshape(tm, -1), wa_ref[...])
    y_b = _dot(yb_ref[...].reshape(tm, -1), wb_ref[...])
    merged = _sigmoid(ga_ref[...].reshape(tm, d)) * y_a + _sigmoid(gb_ref[...].reshape(tm, d)) * y_b
    z = _dot(merged.astype(BF16), wo_ref[...]).reshape(tb, tg, d)
    x1 = _layer_norm(alpha * x_ref[...] + (1.0 + gm_ref[...]) * z, l1g_ref[...], l1b_ref[...])
    h = (x1 * (1.0 + scf_ref[...]) + shf_ref[...]).reshape(tm, d).astype(BF16)

    @pl.when(first)
    def _():
        carry_scr[...] = jnp.zeros_like(carry_scr)

    tt = lax.broadcasted_iota(jnp.int32, (tb, tg, tf), 1)
    for c in range(dff // tf):
        cols = slice(c * tf, (c + 1) * tf)
        a = _dot(h, wu_ref[:, cols])
        gate = _dot(h, wu_ref[:, dff + c * tf:dff + (c + 1) * tf])
        a3 = a.reshape(tb, tg, tf)
        tail = carry_scr[:, :, cols]
        p0 = jnp.where(first, prev_ref[:, 0:1, cols], tail[:, SUBLANES - 2:SUBLANES - 1, :])
        p1 = jnp.where(first, prev_ref[:, 1:2, cols], tail[:, SUBLANES - 1:SUBLANES, :])
        am1 = jnp.where(tt == 0, p1, pltpu.roll(a, 1, 0).reshape(tb, tg, tf))
        am2 = jnp.where(tt == 0, p0, jnp.where(tt == 1, p1, pltpu.roll(a, 2, 0).reshape(tb, tg, tf)))
        conv = cb_ref[:, cols] + cw_ref[0:1, cols] * am2 + cw_ref[1:2, cols] * am1 + cw_ref[2:3, cols] * a3
        act_scr[:, cols] = (_gelu(conv) * gate.reshape(tb, tg, tf)).astype(BF16).reshape(tm, tf)
        carry_scr[:, :, cols] = a3[:, tg - SUBLANES:, :]
        cnew_ref[:, 0, :, cols] = a3[:, tg - 2:, :]

    y = _dot(act_scr[...], wd_ref[...]).reshape(tb, tg, d)
    x2_ref[...] = _layer_norm(alpha * x1 + (1.0 + gf_ref[...]) * y, l2g_ref[...], l2b_ref[...])


def _mixffn(attn, yb, proj, x3, ada3, prev, w_br_a, w_br_b, w_o, ln1_g, ln1_b, w_up, conv_w, conv_b, w_down,
            ln2_g, ln2_b, alpha, gate_cols, rows=512, tf=256):
    nb, g, d = x3.shape
    dff = w_down.shape[0]
    tb, tg = _row_tiles(nb, g, rows)
    ca, cb = gate_cols
    tok = lambda w, c: pl.BlockSpec((tb, tg, w), lambda i, t: (i, t, c))
    mod = lambda c: pl.BlockSpec((tb, 1, d), lambda i, t: (i, 0, c))
    resident = lambda shape: pl.BlockSpec(shape, lambda i, t: (0,) * len(shape), pipeline_mode=pl.Buffered(1))
    vec = lambda a: a.reshape(1, -1)
    return pl.pallas_call(
        functools.partial(_mixffn_kernel, alpha=alpha, tf=tf),
        grid=(nb // tb, g // tg),
        in_specs=[tok(attn.shape[-1], 0), tok(yb.shape[-1], 0), tok(d, ca), tok(d, cb), tok(d, 0),
                  mod(2), mod(3), mod(4), mod(5),
                  resident(w_br_a.shape), resident(w_br_b.shape), resident(w_o.shape),
                  resident((1, d)), resident((1, d)),
                  resident(w_up.shape), resident(conv_w.shape), resident((1, dff)), resident(w_down.shape),
                  resident((1, d)), resident((1, d)),
                  pl.BlockSpec((tb, 2, dff), lambda i, t: (i, 0, 0))],
        out_specs=[tok(d, 0),
                   pl.BlockSpec((tb, 1, 2, dff), lambda i, t: (i, t, 0, 0))],
        out_shape=[jax.ShapeDtypeStruct((nb, g, d), F32), jax.ShapeDtypeStruct((nb, g // tg, 2, dff), F32)],
        scratch_shapes=[pltpu.VMEM((tb * tg, dff), BF16), pltpu.VMEM((tb, SUBLANES, dff), F32)],
        compiler_params=_params(("parallel", "arbitrary")),
        name="mixffn",
    )(attn, yb, proj, proj, x3, ada3, ada3, ada3, ada3, w_br_a, w_br_b, w_o, vec(ln1_g), vec(ln1_b),
      w_up, conv_w, vec(conv_b), w_down, vec(ln2_g), vec(ln2_b), prev)


def kernel(x_prompt, x_sample, cache_k, cache_v, cache_logf, state_hgrn, state_conv, page_table, c_prompt, c_sample, ln0_g, ln0_b, w_in, fox_f_bias, hg_lb, hg_norm_g, w_br_a, w_br_b, w_o, ln1_g, ln1_b, w_up, conv_w, conv_b, w_down, ln2_g, ln2_b, w_ada, b_ada):
    bp, tp, d = x_prompt.shape
    bs, ts, _ = x_sample.shape
    depth = w_in.shape[0]
    heads, dh = cache_k.shape[3], cache_k.shape[4]
    hw = heads * dh
    hg_heads, kd, vd = state_hgrn.shape[2:]
    hgw = hg_heads * kd
    dff = w_down.shape[1]
    assert conv_w.shape[1] == 3 and state_conv.shape[2] == 2
    assert kd == LANES and vd == LANES and 2 * dh == LANES and hw == hgw == hg_heads * vd and d == 2 * hw
    alpha = (2 * depth) ** 0.25

    lb_soft = jax.nn.softmax(hg_lb.astype(F32), axis=0)
    lb_all = jnp.cumsum(lb_soft, axis=0) - lb_soft[0]

    o_f = 3 * hw
    o_b = o_f + heads
    o_g = o_b + 4 * hgw
    w_main = jnp.concatenate([w_in[:, :, o_g:], w_in[:, :, o_b:o_g], w_in[:, :, :o_f]], axis=2).astype(BF16)
    w_f = jnp.pad(w_in[:, :, o_f:o_b], ((0, 0), (0, 0), (0, LANES - heads))).astype(BF16)
    f_bias = jnp.pad(fox_f_bias, ((0, 0), (0, LANES - heads))).reshape(depth, 1, LANES)
    w_rest = 2 * d + 4 * hgw
    gate_cols = (0, 1)
    hg_cols = tuple(2 * d // hgw + i for i in range(4))
    qkv_cols = tuple(w_rest // hw + i for i in range(3))
    k_off, v_off = w_rest + hw, w_rest + 2 * hw
    w_br_a16, w_br_b16, w_o16 = w_br_a.astype(BF16), w_br_b.astype(BF16), w_o.astype(BF16)
    w_up16, w_down16 = w_up.astype(BF16), w_down.astype(BF16)
    consts = _aug_constants(heads, dh)

    ada = _ada(jnp.concatenate([c_prompt, c_sample], axis=0), w_ada, b_ada)
    xp = _ln0(x_prompt.reshape(bp * tp, d), ln0_g, ln0_b).reshape(bp, tp, d)
    xs = _ln0(x_sample.reshape(bs * ts, d), ln0_g, ln0_b).reshape(bs, ts, d)
    s0_p = jnp.zeros((1, bp) + state_hgrn.shape[2:], F32)
    conv0_p = jnp.zeros((bp, 2, dff), F32)

    kv_t = [jnp.zeros((depth, bp, hw, tp), F32), jnp.zeros((depth, bp, hw, tp), F32)]
    hg_p = jnp.zeros((depth, bp) + state_hgrn.shape[2:], F32)
    hg_s = jnp.zeros((depth, bs) + state_hgrn.shape[2:], F32)
    outs_p, outs_s = [], []
    for l in range(depth):
        ada_p = ada[l, :bp].reshape(bp, 1, 6 * d)
        ada_s = ada[l, bp:].reshape(bs, 1, 6 * d)

        proj, logf = _inproj(xp, ada_p, w_main[l], w_f[l], f_bias[l], width=w_rest)
        qa, ka, va_t, *kv_t = _attn_prep(xp, ada_p, w_main[l], logf, consts, qkv_cols, heads, dh, l, depth, kv_t)
        attn = _attn_prompt(qa, ka, va_t, heads, dh)
        yb, hg_p = _hgrn(proj, s0_p, 0, lb_all[l], hg_norm_g[l], hg_cols, l, depth, hg_p)
        xp, conv_new = _mixffn(attn, yb, proj, xp, ada_p, conv0_p, w_br_a16[l], w_br_b16[l], w_o16[l],
                               ln1_g[l], ln1_b[l], w_up16[l], conv_w[l], conv_b[l], w_down16[l],
                               ln2_g[l], ln2_b[l], alpha, gate_cols)
        outs_p.append((logf[:, :, :heads], conv_new[:, -1]))

        proj, logf = _inproj(xs, ada_s, w_main[l], w_f[l], f_bias[l], width=w_main.shape[2])
        attn = _attn_sample(proj, logf, cache_k, cache_v, cache_logf, page_table, l, qkv_cols, heads, dh)
        yb, hg_s = _hgrn(proj, state_hgrn, l, lb_all[l], hg_norm_g[l], hg_cols, l, depth, hg_s)
        xs, conv_new = _mixffn(attn, yb, proj, xs, ada_s, state_conv[l], w_br_a16[l], w_br_b16[l], w_o16[l],
                               ln1_g[l], ln1_b[l], w_up16[l], conv_w[l], conv_b[l], w_down16[l],
                               ln2_g[l], ln2_b[l], alpha, gate_cols)
        outs_s.append((proj[:, :, k_off:k_off + hw].reshape(bs, ts, heads, dh),
                       proj[:, :, v_off:v_off + hw].reshape(bs, ts, heads, dh),
                       logf[:, :, :heads], conv_new[:, -1]))

    stack = lambda outs, i: jnp.stack([o[i] for o in outs])
    k_p, v_p = (jnp.transpose(a.reshape(depth, bp, heads, dh, tp), (0, 1, 4, 2, 3)) for a in kv_t)
    return (xp, xs, k_p, v_p, stack(outs_p, 0), hg_p, stack(outs_p, 1),
            stack(outs_s, 0), stack(outs_s, 1), stack(outs_s, 2), hg_s, stack(outs_s, 3))
```

```python
import functools

import numpy as np
import jax
import jax.numpy as jnp
from jax import lax
from jax.experimental import pallas as pl
from jax.experimental.pallas import tpu as pltpu

F32 = jnp.float32
BF16 = jnp.bfloat16

LN_EPS = 1e-5
RMS_EPS = 1e-6
MASK_VALUE = -1e30
LOG2_E = 1.4426950408889634
LANES = 128
SUBLANES = 8
RING = 3
VMEM_LIMIT = 56 * 1024 * 1024

NT_DIMS = (((1,), (1,)), ((), ()))
TN_DIMS = (((0,), (0,)), ((), ()))


def _params(sem):
    return pltpu.CompilerParams(dimension_semantics=sem, vmem_limit_bytes=VMEM_LIMIT)


def _largest_divisor(n, target, mult):
    best = None
    for d in range(1, n + 1):
        if n % d == 0 and d <= target and d % mult == 0:
            best = d
    return n if best is None else best


def _row_tiles(nb, g, target):
    if g >= target:
        return 1, _largest_divisor(g, target, SUBLANES)
    return _largest_divisor(nb, max(1, target // g), 1), g


def _split3(x):
    hi = x.astype(BF16)
    r1 = x - hi.astype(F32)
    mid = r1.astype(BF16)
    lo = (r1 - mid.astype(F32)).astype(BF16)
    return hi, mid, lo


def _dot(a, b):
    return jnp.dot(a, b, preferred_element_type=F32)


def _dot3(a_bf16, x_f32):
    hi, mid, lo = _split3(x_f32)
    return _dot(a_bf16, hi) + _dot(a_bf16, mid) + _dot(a_bf16, lo)


def _layer_norm(x, g, b):
    mu = jnp.mean(x, axis=-1, keepdims=True)
    xc = x - mu
    var = jnp.mean(xc * xc, axis=-1, keepdims=True)
    return xc * lax.rsqrt(var + LN_EPS) * g + b


def _sigmoid(x):
    return 1.0 / (1.0 + jnp.exp(-x))


def _silu(x):
    return x * _sigmoid(x)


def _log_sigmoid(x):
    return jnp.minimum(x, 0.0) - jnp.log1p(jnp.exp(-jnp.abs(x)))


def _tri_lower(n):
    r = lax.broadcasted_iota(jnp.int32, (n, n), 0)
    c = lax.broadcasted_iota(jnp.int32, (n, n), 1)
    return jnp.where(r >= c, 1.0, 0.0).astype(BF16)


def _ln0_kernel(x_ref, g_ref, b_ref, o_ref):
    o_ref[...] = _layer_norm(x_ref[...], g_ref[...], b_ref[...])


def _ln0(x2, g, b):
    n, d = x2.shape
    tm = _largest_divisor(n, 1024, SUBLANES)
    return pl.pallas_call(
        _ln0_kernel,
        grid=(n // tm,),
        in_specs=[pl.BlockSpec((tm, d), lambda i: (i, 0)),
                  pl.BlockSpec((1, d), lambda i: (0, 0)),
                  pl.BlockSpec((1, d), lambda i: (0, 0))],
        out_specs=pl.BlockSpec((tm, d), lambda i: (i, 0)),
        out_shape=jax.ShapeDtypeStruct((n, d), F32),
        compiler_params=_params(("parallel",)),
        name="ln0",
    )(x2, g.reshape(1, d), b.reshape(1, d))


def _ada_kernel(c_ref, w_ref, b_ref, o_ref):
    s = _silu(c_ref[...]).astype(BF16)
    o_ref[0] = _dot(s, w_ref[0].astype(BF16)) + b_ref[0]


def _ada(c_all, w_ada, b_ada):
    depth, d, n6 = w_ada.shape
    nc = c_all.shape[0]
    tn = _largest_divisor(n6, 1536, LANES)
    return pl.pallas_call(
        _ada_kernel,
        grid=(depth, n6 // tn),
        in_specs=[pl.BlockSpec((nc, d), lambda l, j: (0, 0)),
                  pl.BlockSpec((1, d, tn), lambda l, j: (l, 0, j)),
                  pl.BlockSpec((1, 1, tn), lambda l, j: (l, 0, j))],
        out_specs=pl.BlockSpec((1, nc, tn), lambda l, j: (l, 0, j)),
        out_shape=jax.ShapeDtypeStruct((depth, nc, n6), F32),
        compiler_params=_params(("parallel", "parallel")),
        name="ada",
    )(c_all, w_ada, b_ada.reshape(depth, 1, n6))


def _inproj_kernel(x_ref, sh_ref, sc_ref, w_ref, wf_ref, fb_ref, proj_ref, logf_ref, *, tn):
    tb, tg, d = x_ref.shape
    h = (x_ref[...] * (1.0 + sc_ref[...]) + sh_ref[...]).reshape(tb * tg, d).astype(BF16)
    f = _dot(h, wf_ref[...]) + fb_ref[...]
    logf_ref[...] = _log_sigmoid(f).reshape(tb, tg, LANES)
    for c in range(w_ref.shape[1] // tn):
        cols = slice(c * tn, (c + 1) * tn)
        proj_ref[:, :, cols] = _dot(h, w_ref[:, cols]).reshape(tb, tg, tn)


def _inproj(x3, ada3, w_main, w_f, f_bias, width, tn=512, rows=512):
    nb, g, d = x3.shape
    tb, tg = _row_tiles(nb, g, rows)
    resident = lambda shape: pl.BlockSpec(shape, lambda i, t: (0,) * len(shape), pipeline_mode=pl.Buffered(1))
    return pl.pallas_call(
        functools.partial(_inproj_kernel, tn=tn),
        grid=(nb // tb, g // tg),
        in_specs=[pl.BlockSpec((tb, tg, d), lambda i, t: (i, t, 0)),
                  pl.BlockSpec((tb, 1, d), lambda i, t: (i, 0, 0)),
                  pl.BlockSpec((tb, 1, d), lambda i, t: (i, 0, 1)),
                  resident((d, width)),
                  resident((d, LANES)),
                  resident((1, LANES))],
        out_specs=[pl.BlockSpec((tb, tg, width), lambda i, t: (i, t, 0)),
                   pl.BlockSpec((tb, tg, LANES), lambda i, t: (i, t, 0))],
        out_shape=[jax.ShapeDtypeStruct((nb, g, width), F32),
                   jax.ShapeDtypeStruct((nb, g, LANES), F32)],
        compiler_params=_params(("parallel", "parallel")),
        name="inproj",
    )(x3, ada3, ada3, w_main, w_f, f_bias)


def _aug_constants(heads, dh):
    width = heads * LANES
    pq = np.zeros((LANES, width), np.float32)
    pk = np.zeros((LANES, width), np.float32)
    ones_q = np.zeros((1, width), np.float32)
    ones_k = np.zeros((1, width), np.float32)
    own = np.zeros((1, width), np.float32)
    for h in range(heads):
        o_h = (h % 2) * dh
        a_h = h * LANES + (dh - o_h)
        own[0, h * LANES + o_h:h * LANES + o_h + dh] = 1.0
        for i in range(3):
            pq[i * heads + h, a_h + i] = 1.0
            pk[i * heads + h, a_h + 3 + i] = 1.0
            ones_q[0, a_h + 3 + i] = 1.0
            ones_k[0, a_h + i] = 1.0
    return (jnp.asarray(pq, BF16), jnp.asarray(pk, BF16), jnp.asarray(ones_q), jnp.asarray(ones_k),
            jnp.asarray(own))


def _prep_kernel(x_ref, sh_ref, sc_ref, wq_ref, wk_ref, wv_ref, lf_ref, pq_ref, pk_ref, oq_ref, ok_ref, own_ref,
                 *rest, scale):
    qa_ref, ka_ref, va_ref, kt_ref, vt_ref, carry = rest[-6:]
    tc = x_ref.shape[1]
    heads = qa_ref.shape[2] // LANES

    @pl.when(pl.program_id(1) == 0)
    def _():
        carry[...] = jnp.zeros_like(carry)

    h = (x_ref[0] * (1.0 + sc_ref[0]) + sh_ref[0]).astype(BF16)
    q = _dot(h, wq_ref[...])
    k = _dot(h, wk_ref[...])
    v = _dot(h, wv_ref[...])

    lane = lax.broadcasted_iota(jnp.int32, (tc, LANES), 1)

    def pack3(x):
        pieces = [jnp.where(lane < heads, p.astype(F32), 0.0) for p in _split3(x)]
        return (pieces[0] + pltpu.roll(pieces[1], heads, 1) + pltpu.roll(pieces[2], 2 * heads, 1)).astype(BF16)

    parts = _dot(_tri_lower(tc), pack3(lf_ref[0]))
    d = parts + pltpu.roll(parts, LANES - heads, 1) + pltpu.roll(parts, LANES - 2 * heads, 1) + carry[...]
    carry[...] = d[tc - 1:tc, :]
    packed = pack3(d * LOG2_E)
    aux_q = _dot(packed, pq_ref[...]) + oq_ref[...]
    aux_k = ok_ref[...] - _dot(packed, pk_ref[...])
    own = own_ref[...] > 0.5

    def widen(x):
        return jnp.concatenate([x[:, (h // 2) * LANES:(h // 2 + 1) * LANES] for h in range(heads)], axis=1)

    qa_ref[0] = jnp.where(own, widen(q) * (scale * LOG2_E), aux_q).astype(BF16)
    ka_ref[0] = jnp.where(own, widen(k), aux_k).astype(BF16)
    dh = LANES // 2
    row = lax.broadcasted_iota(jnp.int32, (LANES, tc), 0)
    for hp in range(heads // 2):
        pair = slice(hp * LANES, (hp + 1) * LANES)
        kt_ref[0, 0, pair, :] = k[:, pair].T
        vt = v[:, pair].T
        vt_ref[0, 0, pair, :] = vt
        va_ref[0, (2 * hp) * LANES:(2 * hp + 1) * LANES, :] = jnp.where(
            row < dh, vt, jnp.where(row == dh, 1.0, 0.0)).astype(BF16)
        va_ref[0, (2 * hp + 1) * LANES:(2 * hp + 2) * LANES, :] = jnp.where(
            row >= dh, vt, jnp.where(row == 0, 1.0, 0.0)).astype(BF16)


def _attn_prep(x3, ada3, w_main, logf, consts, cols, heads, dh, layer, depth, stacked):
    b, t, d = x3.shape
    hw = heads * dh
    width = heads * LANES
    tc = _largest_divisor(t, 512, SUBLANES)
    pq, pk, oq, ok, own = consts
    cq, ck, cv = cols
    full = lambda shape: pl.BlockSpec(shape, lambda i, j: (0,) * len(shape))
    out = jax.ShapeDtypeStruct((b, t, width), BF16)
    cache = jax.ShapeDtypeStruct((depth, b, hw, t), F32)
    cache_spec = pl.BlockSpec((1, 1, hw, tc), lambda i, j: (layer, i, 0, j))
    extra = tuple(stacked)
    n_in = 12
    return pl.pallas_call(
        functools.partial(_prep_kernel, scale=dh ** -0.5),
        grid=(b, t // tc),
        in_specs=[pl.BlockSpec((1, tc, d), lambda i, j: (i, j, 0)),
                  pl.BlockSpec((1, 1, d), lambda i, j: (i, 0, 0)),
                  pl.BlockSpec((1, 1, d), lambda i, j: (i, 0, 1)),
                  pl.BlockSpec((d, hw), lambda i, j: (0, cq)),
                  pl.BlockSpec((d, hw), lambda i, j: (0, ck)),
                  pl.BlockSpec((d, hw), lambda i, j: (0, cv)),
                  pl.BlockSpec((1, tc, LANES), lambda i, j: (i, j, 0)),
                  full(pq.shape), full(pk.shape), full(oq.shape), full(ok.shape), full(own.shape)]
                 + [pl.BlockSpec(memory_space=pl.ANY)] * len(extra),
        out_specs=[pl.BlockSpec((1, tc, width), lambda i, j: (i, j, 0)),
                   pl.BlockSpec((1, tc, width), lambda i, j: (i, j, 0)),
                   pl.BlockSpec((1, width, tc), lambda i, j: (i, 0, j)),
                   cache_spec, cache_spec],
        out_shape=[out, out, jax.ShapeDtypeStruct((b, width, t), BF16), cache, cache],
        input_output_aliases={n_in + i: 3 + i for i in range(len(extra))},
        scratch_shapes=[pltpu.VMEM((1, LANES), F32)],
        compiler_params=_params(("parallel", "arbitrary")),
        name="attn_prep",
    )(x3, ada3, ada3, w_main, w_main, w_main, logf, pq, pk, oq, ok, own, *extra)


def _attn_kernel(qi_ref, kj_ref, q_ref, k_ref, v_ref, o_ref, m_scr, acc_scr, *, dh):
    tq = q_ref.shape[1]
    tk = k_ref.shape[1]
    heads = q_ref.shape[2] // LANES
    step = pl.program_id(1)
    i = qi_ref[step]
    j = kj_ref[step]

    @pl.when(j == 0)
    def _():
        m_scr[...] = jnp.full_like(m_scr, MASK_VALUE)
        acc_scr[...] = jnp.zeros_like(acc_scr)

    def update(masked):
        if masked:
            key = lax.broadcasted_iota(jnp.int32, (tk, tq), 0)
            qry = lax.broadcasted_iota(jnp.int32, (tk, tq), 1)
            keep = key <= qry

        def scores(h):
            blk = slice(h * LANES, (h + 1) * LANES)
            s = lax.dot_general(k_ref[0, :, blk], q_ref[0, :, blk], NT_DIMS, preferred_element_type=F32)
            if masked:
                s = jnp.where(keep, s, MASK_VALUE)
            m_prev = m_scr[h]
            return s, m_prev, jnp.maximum(m_prev, jnp.max(s, axis=0, keepdims=True))

        def accumulate(h, s, m_prev, m_new):
            p = jnp.exp2(s - m_new).astype(BF16)
            acc_scr[h] = jnp.exp2(m_prev - m_new) * acc_scr[h] + _dot(v_ref[0, h * LANES:(h + 1) * LANES, :], p)
            m_scr[h] = m_new

        ahead = 2
        pending = [scores(h) for h in range(ahead)]
        for h in range(heads):
            if h + ahead < heads:
                pending.append(scores(h + ahead))
            accumulate(h, *pending.pop(0))

    @pl.when(j < i)
    def _():
        update(False)

    @pl.when(j == i)
    def _():
        update(True)
        row = lax.broadcasted_iota(jnp.int32, (LANES, tq), 0)
        for hp in range(heads // 2):
            a0 = acc_scr[2 * hp]
            a1 = acc_scr[2 * hp + 1]
            o_t = jnp.where(row < dh, a0 / a0[dh:dh + 1, :], a1 / a1[0:1, :])
            o_ref[0, :, hp * LANES:(hp + 1) * LANES] = o_t.T.astype(o_ref.dtype)


def _attn_prompt(qa, ka, va, heads, dh):
    b, t, width = qa.shape
    tq = _largest_divisor(t, 512, SUBLANES)
    nq = t // tq
    qi = np.array([i for i in range(nq) for j in range(i + 1)], np.int32)
    kj = np.array([j for i in range(nq) for j in range(i + 1)], np.int32)
    grid_spec = pltpu.PrefetchScalarGridSpec(
        num_scalar_prefetch=2,
        grid=(b, len(qi)),
        in_specs=[pl.BlockSpec((1, tq, width), lambda bb, s, qi_r, kj_r: (bb, qi_r[s], 0)),
                  pl.BlockSpec((1, tq, width), lambda bb, s, qi_r, kj_r: (bb, kj_r[s], 0)),
                  pl.BlockSpec((1, width, tq), lambda bb, s, qi_r, kj_r: (bb, 0, kj_r[s]))],
        out_specs=pl.BlockSpec((1, tq, heads * dh), lambda bb, s, qi_r, kj_r: (bb, qi_r[s], 0)),
        scratch_shapes=[pltpu.VMEM((heads, 1, tq), F32), pltpu.VMEM((heads, LANES, tq), F32)],
    )
    return pl.pallas_call(
        functools.partial(_attn_kernel, dh=dh),
        grid_spec=grid_spec,
        out_shape=jax.ShapeDtypeStruct((b, t, heads * dh), BF16),
        compiler_params=_params(("parallel", "arbitrary")),
        name="attn_prompt",
    )(jnp.asarray(qi), jnp.asarray(kj), qa, ka, va)


def _attn_sample_kernel(pt_ref, q_ref, kc_ref, vc_ref, lfc_ref, ck_hbm, cv_hbm, clf_hbm, o_ref,
                        kbuf, vbuf, lbuf, sems, *, layer, n_pages, heads, dh):
    b = pl.program_id(0)
    nb = pl.num_programs(0)
    page = lbuf.shape[2]
    hw = heads * dh
    t_cur = q_ref.shape[1]
    rows = heads * t_cur
    npr = n_pages * heads

    def copies(seq, slot):
        out = []
        for p in range(n_pages):
            pid = pt_ref[seq, p]
            lanes = pl.ds(p * page, page)
            out.append(pltpu.make_async_copy(ck_hbm.at[layer, pid], kbuf.at[slot, :, lanes], sems.at[0, slot, p]))
            out.append(pltpu.make_async_copy(cv_hbm.at[layer, pid], vbuf.at[slot, :, lanes], sems.at[1, slot, p]))
            out.append(pltpu.make_async_copy(clf_hbm.at[layer, pid], lbuf.at[slot, pl.ds(p * heads, heads), :],
                                             sems.at[2, slot, p]))
        return out

    @pl.when(b == 0)
    def _():
        for ahead in range(RING - 1):
            @pl.when(ahead < nb)
            def _():
                for c in copies(ahead, ahead):
                    c.start()

    slot = b % RING

    @pl.when(b + RING - 1 < nb)
    def _():
        for c in copies(b + RING - 1, (b + RING - 1) % RING):
            c.start()

    for c in copies(b, slot):
        c.wait()

    ur = lax.broadcasted_iota(jnp.int32, (page, page), 0)
    uc = lax.broadcasted_iota(jnp.int32, (page, page), 1)
    triu = jnp.where(ur <= uc, 1.0, 0.0).astype(BF16)
    l_hi, l_mid, l_lo = _split3(lbuf[slot])
    cum = _dot(l_hi, triu) + _dot(l_mid, triu) + _dot(l_lo, triu)
    mr = lax.broadcasted_iota(jnp.int32, (npr, npr), 0)
    mc = lax.broadcasted_iota(jnp.int32, (npr, npr), 1)
    later = jnp.where((mr % heads == mc % heads) & (mc // heads >= mr // heads), 1.0, 0.0).astype(BF16)
    e = cum - _dot3(later, jnp.broadcast_to(cum[:, page - 1:page], (npr, page)))
    rr = lax.broadcasted_iota(jnp.int32, (n_pages * rows, npr), 0)
    rc = lax.broadcasted_iota(jnp.int32, (n_pages * rows, npr), 1)
    spread = jnp.where((rr // rows == rc // heads) & ((rr % rows) // t_cur == rc % heads), -1.0, 0.0).astype(BF16)
    bias_rows = _dot3(spread, e)
    bias = jnp.concatenate([bias_rows[p * rows:(p + 1) * rows] for p in range(n_pages)], axis=1)

    cc = _dot3(_tri_lower(t_cur), lfc_ref[0])
    rrow = lax.broadcasted_iota(jnp.int32, (rows, hw), 0) // t_cur
    rcol = lax.broadcasted_iota(jnp.int32, (rows, hw), 1) // dh
    q_rep = jnp.concatenate([q_ref[0]] * heads, axis=0)
    qbd = jnp.where(rrow == rcol, q_rep * (dh ** -0.5), 0.0).astype(BF16)
    arow = lax.broadcasted_iota(jnp.int32, (rows, LANES), 0) // t_cur
    acol = lax.broadcasted_iota(jnp.int32, (rows, LANES), 1)
    sel = arow == acol
    neg_sel = jnp.where(sel, -1.0, 0.0).astype(BF16)
    cc_rep = jnp.concatenate([cc] * heads, axis=0)
    dq = jnp.sum(jnp.where(sel, cc_rep, 0.0), axis=-1, keepdims=True)

    nt = lambda a, bm: lax.dot_general(a, bm, NT_DIMS, preferred_element_type=F32)
    s_past = _dot(qbd, kbuf[slot].astype(BF16)) + bias + dq
    c_hi, c_mid, c_lo = _split3(cc)
    s_cur = nt(qbd, kc_ref[0].astype(BF16)) + (nt(neg_sel, c_hi) + nt(neg_sel, c_mid) + nt(neg_sel, c_lo)) + dq
    tq_idx = lax.broadcasted_iota(jnp.int32, (rows, t_cur), 0) % t_cur
    ts_idx = lax.broadcasted_iota(jnp.int32, (rows, t_cur), 1)
    s_cur = jnp.where(ts_idx <= tq_idx, s_cur, MASK_VALUE)

    m = jnp.maximum(jnp.max(s_past, axis=-1, keepdims=True), jnp.max(s_cur, axis=-1, keepdims=True))
    p_past = jnp.exp(s_past - m)
    p_cur = jnp.exp(s_cur - m)
    denom = jnp.sum(p_past, axis=-1, keepdims=True) + jnp.sum(p_cur, axis=-1, keepdims=True)
    o_all = (nt(p_past.astype(BF16), vbuf[slot].astype(BF16))
             + _dot(p_cur.astype(BF16), vc_ref[0].astype(BF16))) / denom
    o_all = jnp.where(rrow == rcol, o_all, 0.0)
    o = o_all[0:t_cur]
    for h in range(1, heads):
        o = o + o_all[h * t_cur:(h + 1) * t_cur]
    o_ref[0] = o.astype(o_ref.dtype)


def _attn_sample(proj, logf, cache_k, cache_v, cache_logf, page_table, layer, cols, heads, dh):
    nb, t_cur, _ = proj.shape
    hw = heads * dh
    depth, n_pool, page = cache_k.shape[:3]
    n_pages = page_table.shape[1]
    ck = jnp.transpose(cache_k, (0, 1, 3, 4, 2)).reshape(depth, n_pool, hw, page)
    cv = jnp.transpose(cache_v, (0, 1, 3, 4, 2)).reshape(depth, n_pool, hw, page)
    clf = jnp.transpose(cache_logf, (0, 1, 3, 2))
    cq, ckc, cvc = cols
    grid_spec = pltpu.PrefetchScalarGridSpec(
        num_scalar_prefetch=1,
        grid=(nb,),
        in_specs=[pl.BlockSpec((1, t_cur, hw), lambda b, pt: (b, 0, cq)),
                  pl.BlockSpec((1, t_cur, hw), lambda b, pt: (b, 0, ckc)),
                  pl.BlockSpec((1, t_cur, hw), lambda b, pt: (b, 0, cvc)),
                  pl.BlockSpec((1, t_cur, LANES), lambda b, pt: (b, 0, 0)),
                  pl.BlockSpec(memory_space=pl.ANY),
                  pl.BlockSpec(memory_space=pl.ANY),
                  pl.BlockSpec(memory_space=pl.ANY)],
        out_specs=pl.BlockSpec((1, t_cur, hw), lambda b, pt: (b, 0, 0)),
        scratch_shapes=[pltpu.VMEM((RING, hw, n_pages * page), F32),
                        pltpu.VMEM((RING, hw, n_pages * page), F32),
                        pltpu.VMEM((RING, n_pages * heads, page), F32),
                        pltpu.SemaphoreType.DMA((3, RING, n_pages))],
    )
    return pl.pallas_call(
        functools.partial(_attn_sample_kernel, layer=layer, n_pages=n_pages, heads=heads, dh=dh),
        grid_spec=grid_spec,
        out_shape=jax.ShapeDtypeStruct((nb, t_cur, hw), BF16),
        compiler_params=_params(("arbitrary",)),
        name="attn_sample",
    )(page_table, proj, proj, proj, logf, ck, cv, clf)


def _hgrn_kernel(q_ref, f_ref, v_ref, og_ref, lb_ref, ng_ref, s0_ref, *rest, chunk, sub):
    y_ref, sout_ref, st_scr = rest[-3:]
    tb, tt, width = q_ref.shape
    heads = s0_ref.shape[2]
    kd = width // heads
    n_chunks = tt // chunk
    n_sub = chunk // sub
    t_idx = pl.program_id(1)
    hb = [slice(h * kd, (h + 1) * kd) for h in range(heads)]
    per_head = lambda fn: jnp.concatenate([fn(h) for h in range(heads)], axis=1)

    @pl.when(t_idx == 0)
    def _():
        for ib in range(tb):
            for h in range(heads):
                st_scr[ib, h] = s0_ref[0, ib, h].T

    lb = lb_ref[...]
    ng = ng_ref[...]
    tri = _tri_lower(chunk)
    sub_col = lax.broadcasted_iota(jnp.int32, (sub, 1), 0)
    half = sub // 2
    upper = lax.broadcasted_iota(jnp.int32, (chunk, 1), 0) % sub >= half
    same_sub = (lax.broadcasted_iota(jnp.int32, (chunk, chunk), 0) // sub
                == lax.broadcasted_iota(jnp.int32, (chunk, chunk), 1) // sub)

    nt = lambda x, y: lax.dot_general(x, y, NT_DIMS, preferred_element_type=F32)

    def prep(ib, rows):
        c = {"ib": ib, "rows": rows}
        g = lb + (1.0 - lb) * _sigmoid(f_ref[ib, rows, :])
        kk = 1.0 - g
        qq = _silu(q_ref[ib, rows, :])
        v = v_ref[ib, rows, :]
        bcum = _dot3(tri, jnp.log(g))
        b_last = bcum[chunk - 1:chunk]
        b2 = bcum * LOG2_E
        c.update(kk=kk, qq=qq, v=v, vb=v.astype(BF16), b2=b2, e_last=jnp.exp(b_last),
                 qe=(qq * jnp.exp(bcum)).astype(BF16), kdec=(kk * jnp.exp(b_last - bcum)).astype(BF16))
        cross = []
        for si in range(1, n_sub):
            lo_r, hi_r = si * sub, (si + 1) * sub
            ref = bcum[lo_r - 1:lo_r]
            cross.append(((qq[lo_r:hi_r] * jnp.exp(bcum[lo_r:hi_r] - ref)).astype(BF16),
                          (kk[:lo_r] * jnp.exp(ref - bcum[:lo_r])).astype(BF16)))
        mid = jnp.concatenate([jnp.broadcast_to(b2[si * sub + half - 1:si * sub + half], (sub, width))
                               for si in range(n_sub)], axis=0)
        c.update(cross=cross, a_mid=jnp.where(upper, qq * jnp.exp2(b2 - mid), 0.0).astype(BF16),
                 b_mid=jnp.where(upper, 0.0, kk * jnp.exp2(mid - b2)).astype(BF16))
        return c

    def stage1(c, st):
        c["o_inter"] = [nt(c["qe"][:, hb[h]], st[h].astype(BF16)) for h in range(heads)]
        st_add = [lax.dot_general(c["vb"][:, hb[h]], c["kdec"][:, hb[h]], TN_DIMS, preferred_element_type=F32)
                  for h in range(heads)]
        c["cross_sc"] = [[nt(a[:, hb[h]], bm[:, hb[h]]) for h in range(heads)] for a, bm in c["cross"]]
        c["mid_sc"] = [nt(c["a_mid"][:, hb[h]], c["b_mid"][:, hb[h]]) for h in range(heads)]
        return [c["e_last"][:, hb[h]] * st[h] + st_add[h] for h in range(heads)]

    def pairwise(c):
        diag = []
        for si in range(n_sub):
            lo_r, hi_r = si * sub, (si + 1) * sub
            qs, ks, vs, bs2 = c["qq"][lo_r:hi_r], c["kk"][lo_r:hi_r], c["v"][lo_r:hi_r], c["b2"][lo_r:hi_r]
            acc = None
            for dlt in range(half):
                k_s, v_s, b_s = (x if dlt == 0 else pltpu.roll(x, dlt, 0) for x in (ks, vs, bs2))
                y = qs * k_s * jnp.exp2(bs2 - b_s)
                term = per_head(lambda h: jnp.where(sub_col % half >= dlt,
                                                    jnp.sum(y[:, hb[h]], axis=-1, keepdims=True), 0.0) * v_s[:, hb[h]])
                acc = term if acc is None else acc + term
            diag.append(acc)
        return jnp.concatenate(diag, axis=0) if n_sub > 1 else diag[0]

    def stage2(c):
        vb = c["vb"]
        o = per_head(lambda h: c["o_inter"][h] + _dot(jnp.where(same_sub, c["mid_sc"][h], 0.0).astype(BF16),
                                                      vb[:, hb[h]]))
        if n_sub > 1:
            zero = jnp.zeros((sub, width), F32)
            o = o + jnp.concatenate([zero] + [per_head(lambda h, si=si: _dot(c["cross_sc"][si - 1][h].astype(BF16),
                                                                            vb[:si * sub, hb[h]]))
                                              for si in range(1, n_sub)], axis=0)
        return o

    def finish(c, o):
        o = per_head(lambda h: o[:, hb[h]] * lax.rsqrt(jnp.mean(o[:, hb[h]] * o[:, hb[h]], axis=-1, keepdims=True)
                                                       + RMS_EPS))
        y_ref[c["ib"], c["rows"], :] = (o * ng * _silu(og_ref[c["ib"], c["rows"], :])).astype(y_ref.dtype)

    def do_chunks(ib, row_slices):
        ctxs = [prep(ib, rows) for rows in row_slices]
        st = [st_scr[ib, h] for h in range(heads)]
        for c in ctxs:
            st = stage1(c, st)
        outs = []
        for c in ctxs:
            d = pairwise(c)
            outs.append(stage2(c) + d)
        for c, o in zip(ctxs, outs):
            finish(c, o)
        for h in range(heads):
            st_scr[ib, h] = st[h]

    group = 4 if n_chunks % 4 == 0 else 1
    for ib in range(tb):
        if n_chunks == group:
            do_chunks(ib, [slice(k * chunk, (k + 1) * chunk) for k in range(group)])
        else:
            def body(c, carry, ib=ib):
                base = c * (group * chunk)
                do_chunks(ib, [pl.ds(pl.multiple_of(base + k * chunk, chunk), chunk) for k in range(group)])
                return carry
            lax.fori_loop(0, n_chunks // group, body, 0)

    @pl.when(t_idx == pl.num_programs(1) - 1)
    def _():
        for ib in range(tb):
            for h in range(heads):
                sout_ref[0, ib, h] = st_scr[ib, h].T


def _hgrn(proj, s0, l_in, lb, norm_g, cols, layer, depth, stacked):
    nb, g, _ = proj.shape
    hg_heads, kd, vd = s0.shape[2:]
    width = hg_heads * kd
    extra = (stacked,)
    if g >= 64:
        tb, tt = 1, _largest_divisor(g, 512, 64)
        chunk, sub = 64, 8
    else:
        tb, tt = _largest_divisor(nb, 8, 1), g
        chunk, sub = g, g
    col = lambda c0: pl.BlockSpec((tb, tt, width), lambda i, t: (i, t, c0))
    return pl.pallas_call(
        functools.partial(_hgrn_kernel, chunk=chunk, sub=sub),
        grid=(nb // tb, g // tt),
        in_specs=[col(cols[0]), col(cols[1]), col(cols[2]), col(cols[3]),
                  pl.BlockSpec((1, width), lambda i, t: (0, 0)),
                  pl.BlockSpec((1, width), lambda i, t: (0, 0)),
                  pl.BlockSpec((1, tb, hg_heads, kd, vd), lambda i, t: (l_in, i, 0, 0, 0))]
                 + [pl.BlockSpec(memory_space=pl.ANY)] * len(extra),
        out_specs=[pl.BlockSpec((tb, tt, width), lambda i, t: (i, t, 0)),
                   pl.BlockSpec((1, tb, hg_heads, kd, vd), lambda i, t: (layer, i, 0, 0, 0))],
        out_shape=[jax.ShapeDtypeStruct((nb, g, width), BF16),
                   jax.ShapeDtypeStruct((depth, nb, hg_heads, kd, vd), F32)],
        input_output_aliases={7 + i: 1 for i in range(len(extra))},
        scratch_shapes=[pltpu.VMEM((tb, hg_heads, vd, kd), F32)],
        compiler_params=_params(("parallel", "arbitrary")),
        name="hgrn",
    )(proj, proj, proj, proj, lb.reshape(1, -1), norm_g.reshape(1, -1), s0, *extra)


def _gelu(x):
    return 0.5 * x * (1.0 + lax.erf(x * (2.0 ** -0.5)))


def _mixffn_kernel(ya_ref, yb_ref, ga_ref, gb_ref, x_ref, gm_ref, shf_ref, scf_ref, gf_ref,
                   wa_ref, wb_ref, wo_ref, l1g_ref, l1b_ref, wu_ref, cw_ref, cb_ref, wd_ref, l2g_ref, l2b_ref,
                   prev_ref, x2_ref, cnew_ref, act_scr, carry_scr, *, alpha, tf):
    tb, tg, d = x_ref.shape
    tm = tb * tg
    dff = wd_ref.shape[0]
    first = pl.program_id(1) == 0

    y_a = _dot(ya_ref[...].reshape(tm, -1), wa_ref[...])
    y_b = _dot(yb_ref[...].reshape(tm, -1), wb_ref[...])
    merged = _sigmoid(ga_ref[...].reshape(tm, d)) * y_a + _sigmoid(gb_ref[...].reshape(tm, d)) * y_b
    z = _dot(merged.astype(BF16), wo_ref[...]).reshape(tb, tg, d)
    x1 = _layer_norm(alpha * x_ref[...] + (1.0 + gm_ref[...]) * z, l1g_ref[...], l1b_ref[...])
    h = (x1 * (1.0 + scf_ref[...]) + shf_ref[...]).reshape(tm, d).astype(BF16)

    @pl.when(first)
    def _():
        carry_scr[...] = jnp.zeros_like(carry_scr)

    tt = lax.broadcasted_iota(jnp.int32, (tb, tg, tf), 1)
    for c in range(dff // tf):
        cols = slice(c * tf, (c + 1) * tf)
        a = _dot(h, wu_ref[:, cols])
        gate = _dot(h, wu_ref[:, dff + c * tf:dff + (c + 1) * tf])
        a3 = a.reshape(tb, tg, tf)
        tail = carry_scr[:, :, cols]
        p0 = jnp.where(first, prev_ref[:, 0:1, cols], tail[:, SUBLANES - 2:SUBLANES - 1, :])
        p1 = jnp.where(first, prev_ref[:, 1:2, cols], tail[:, SUBLANES - 1:SUBLANES, :])
        am1 = jnp.where(tt == 0, p1, pltpu.roll(a, 1, 0).reshape(tb, tg, tf))
        am2 = jnp.where(tt == 0, p0, jnp.where(tt == 1, p1, pltpu.roll(a, 2, 0).reshape(tb, tg, tf)))
        conv = cb_ref[:, cols] + cw_ref[0:1, cols] * am2 + cw_ref[1:2, cols] * am1 + cw_ref[2:3, cols] * a3
        act_scr[:, cols] = (_gelu(conv) * gate.reshape(tb, tg, tf)).astype(BF16).reshape(tm, tf)
        carry_scr[:, :, cols] = a3[:, tg - SUBLANES:, :]
        cnew_ref[:, 0, :, cols] = a3[:, tg - 2:, :]

    y = _dot(act_scr[...], wd_ref[...]).reshape(tb, tg, d)
    x2_ref[...] = _layer_norm(alpha * x1 + (1.0 + gf_ref[...]) * y, l2g_ref[...], l2b_ref[...])


def _mixffn(attn, yb, proj, x3, ada3, prev, w_br_a, w_br_b, w_o, ln1_g, ln1_b, w_up, conv_w, conv_b, w_down,
            ln2_g, ln2_b, alpha, gate_cols, rows=512, tf=256):
    nb, g, d = x3.shape
    dff = w_down.shape[0]
    tb, tg = _row_tiles(nb, g, rows)
    ca, cb = gate_cols
    tok = lambda w, c: pl.BlockSpec((tb, tg, w), lambda i, t: (i, t, c))
    mod = lambda c: pl.BlockSpec((tb, 1, d), lambda i, t: (i, 0, c))
    resident = lambda shape: pl.BlockSpec(shape, lambda i, t: (0,) * len(shape), pipeline_mode=pl.Buffered(1))
    vec = lambda a: a.reshape(1, -1)
    return pl.pallas_call(
        functools.partial(_mixffn_kernel, alpha=alpha, tf=tf),
        grid=(nb // tb, g // tg),
        in_specs=[tok(attn.shape[-1], 0), tok(yb.shape[-1], 0), tok(d, ca), tok(d, cb), tok(d, 0),
                  mod(2), mod(3), mod(4), mod(5),
                  resident(w_br_a.shape), resident(w_br_b.shape), resident(w_o.shape),
                  resident((1, d)), resident((1, d)),
                  resident(w_up.shape), resident(conv_w.shape), resident((1, dff)), resident(w_down.shape),
                  resident((1, d)), resident((1, d)),
                  pl.BlockSpec((tb, 2, dff), lambda i, t: (i, 0, 0))],
        out_specs=[tok(d, 0),
                   pl.BlockSpec((tb, 1, 2, dff), lambda i, t: (i, t, 0, 0))],
        out_shape=[jax.ShapeDtypeStruct((nb, g, d), F32), jax.ShapeDtypeStruct((nb, g // tg, 2, dff), F32)],
        scratch_shapes=[pltpu.VMEM((tb * tg, dff), BF16), pltpu.VMEM((tb, SUBLANES, dff), F32)],
        compiler_params=_params(("parallel", "arbitrary")),
        name="mixffn",
    )(attn, yb, proj, proj, x3, ada3, ada3, ada3, ada3, w_br_a, w_br_b, w_o, vec(ln1_g), vec(ln1_b),
      w_up, conv_w, vec(conv_b), w_down, vec(ln2_g), vec(ln2_b), prev)


def kernel(x_prompt, x_sample, cache_k, cache_v, cache_logf, state_hgrn, state_conv, page_table, c_prompt, c_sample, ln0_g, ln0_b, w_in, fox_f_bias, hg_lb, hg_norm_g, w_br_a, w_br_b, w_o, ln1_g, ln1_b, w_up, conv_w, conv_b, w_down, ln2_g, ln2_b, w_ada, b_ada):
    bp, tp, d = x_prompt.shape
    bs, ts, _ = x_sample.shape
    depth = w_in.shape[0]
    heads, dh = cache_k.shape[3], cache_k.shape[4]
    hw = heads * dh
    hg_heads, kd, vd = state_hgrn.shape[2:]
    hgw = hg_heads * kd
    dff = w_down.shape[1]
    assert conv_w.shape[1] == 3 and state_conv.shape[2] == 2
    assert kd == LANES and vd == LANES and 2 * dh == LANES and hw == hgw == hg_heads * vd and d == 2 * hw
    alpha = (2 * depth) ** 0.25

    lb_soft = jax.nn.softmax(hg_lb.astype(F32), axis=0)
    lb_all = jnp.cumsum(lb_soft, axis=0) - lb_soft[0]

    o_f = 3 * hw
    o_b = o_f + heads
    o_g = o_b + 4 * hgw
    w_main = jnp.concatenate([w_in[:, :, o_g:], w_in[:, :, o_b:o_g], w_in[:, :, :o_f]], axis=2).astype(BF16)
    w_f = jnp.pad(w_in[:, :, o_f:o_b], ((0, 0), (0, 0), (0, LANES - heads))).astype(BF16)
    f_bias = jnp.pad(fox_f_bias, ((0, 0), (0, LANES - heads))).reshape(depth, 1, LANES)
    w_rest = 2 * d + 4 * hgw
    gate_cols = (0, 1)
    hg_cols = tuple(2 * d // hgw + i for i in range(4))
    qkv_cols = tuple(w_rest // hw + i for i in range(3))
    k_off, v_off = w_rest + hw, w_rest + 2 * hw
    w_br_a16, w_br_b16, w_o16 = w_br_a.astype(BF16), w_br_b.astype(BF16), w_o.astype(BF16)
    w_up16, w_down16 = w_up.astype(BF16), w_down.astype(BF16)
    consts = _aug_constants(heads, dh)

    ada = _ada(jnp.concatenate([c_prompt, c_sample], axis=0), w_ada, b_ada)
    xp = _ln0(x_prompt.reshape(bp * tp, d), ln0_g, ln0_b).reshape(bp, tp, d)
    xs = _ln0(x_sample.reshape(bs * ts, d), ln0_g, ln0_b).reshape(bs, ts, d)
    s0_p = jnp.zeros((1, bp) + state_hgrn.shape[2:], F32)
    conv0_p = jnp.zeros((bp, 2, dff), F32)

    kv_t = [jnp.zeros((depth, bp, hw, tp), F32), jnp.zeros((depth, bp, hw, tp), F32)]
    hg_p = jnp.zeros((depth, bp) + state_hgrn.shape[2:], F32)
    hg_s = jnp.zeros((depth, bs) + state_hgrn.shape[2:], F32)
    outs_p, outs_s = [], []
    for l in range(depth):
        ada_p = ada[l, :bp].reshape(bp, 1, 6 * d)
        ada_s = ada[l, bp:].reshape(bs, 1, 6 * d)

        proj, logf = _inproj(xp, ada_p, w_main[l], w_f[l], f_bias[l], width=w_rest)
        qa, ka, va_t, *kv_t = _attn_prep(xp, ada_p, w_main[l], logf, consts, qkv_cols, heads, dh, l, depth, kv_t)
        attn = _attn_prompt(qa, ka, va_t, heads, dh)
        yb, hg_p = _hgrn(proj, s0_p, 0, lb_all[l], hg_norm_g[l], hg_cols, l, depth, hg_p)
        xp, conv_new = _mixffn(attn, yb, proj, xp, ada_p, conv0_p, w_br_a16[l], w_br_b16[l], w_o16[l],
                               ln1_g[l], ln1_b[l], w_up16[l], conv_w[l], conv_b[l], w_down16[l],
                               ln2_g[l], ln2_b[l], alpha, gate_cols)
        outs_p.append((logf[:, :, :heads], conv_new[:, -1]))

        proj, logf = _inproj(xs, ada_s, w_main[l], w_f[l], f_bias[l], width=w_main.shape[2])
        attn = _attn_sample(proj, logf, cache_k, cache_v, cache_logf, page_table, l, qkv_cols, heads, dh)
        yb, hg_s = _hgrn(proj, state_hgrn, l, lb_all[l], hg_norm_g[l], hg_cols, l, depth, hg_s)
        xs, conv_new = _mixffn(attn, yb, proj, xs, ada_s, state_conv[l], w_br_a16[l], w_br_b16[l], w_o16[l],
                               ln1_g[l], ln1_b[l], w_up16[l], conv_w[l], conv_b[l], w_down16[l],
                               ln2_g[l], ln2_b[l], alpha, gate_cols)
        outs_s.append((proj[:, :, k_off:k_off + hw].reshape(bs, ts, heads, dh),
                       proj[:, :, v_off:v_off + hw].reshape(bs, ts, heads, dh),
                       logf[:, :, :heads], conv_new[:, -1]))

    stack = lambda outs, i: jnp.stack([o[i] for o in outs])
    k_p, v_p = (jnp.transpose(a.reshape(depth, bp, heads, dh, tp), (0, 1, 4, 2, 3)) for a in kv_t)
    return (xp, xs, k_p, v_p, stack(outs_p, 0), hg_p, stack(outs_p, 1),
            stack(outs_s, 0), stack(outs_s, 1), stack(outs_s, 2), hg_s, stack(outs_s, 3))
```
